```python
import math
import jax, jax.numpy as jnp
from jax import lax
import numpy as np

D_MODEL = 1024
BATCH = 4
SEQ = 4096
DEPTH = 4

N_MIXERS = 3
N_HEADS = 8
HEAD_DIM = D_MODEL // N_HEADS
ROT_DIM = HEAD_DIM // 4
ROPE_THETA = 500000.0
MOBA_BLOCK = 256
MOBA_TOPK = 3
Q_CHUNK = 16
POOL_WINDOWS = (2, 4, 8, 16)
N_POOL_GROUPS = len(POOL_WINDOWS)
POOL_GROUP_DIM = D_MODEL // N_POOL_GROUPS
CONV_WIDTH = 3
D_FF = 4 * D_MODEL
NORM_EPS = 1e-6
NEG_INF = -1e30

kernel_name = "hybrid_moba_pool_shortconv_trunk"


def rms_norm(x, g):
    xf = x.astype(jnp.float32)
    y = xf * lax.rsqrt(jnp.mean(xf * xf, axis=-1, keepdims=True) + NORM_EPS)
    return (y * g.astype(jnp.float32)).astype(x.dtype)


def rope_tables(positions):
    inv_freq = ROPE_THETA ** (-jnp.arange(0, ROT_DIM, 2, dtype=jnp.float32) / ROT_DIM)
    ang = positions.astype(jnp.float32)[..., None] * inv_freq
    return jnp.cos(ang)[:, None], jnp.sin(ang)[:, None]


def apply_partial_rope(t, cos, sin):
    half = ROT_DIM // 2
    rot = t[..., :ROT_DIM].astype(jnp.float32)
    x1, x2 = rot[..., :half], rot[..., half:]
    rotated = jnp.concatenate([x1 * cos - x2 * sin, x2 * cos + x1 * sin], axis=-1)
    return jnp.concatenate([rotated.astype(t.dtype), t[..., ROT_DIM:]], axis=-1)


def moba_attention(xn, w_qkv, w_o, cos, sin):
    b, s, _ = xn.shape
    qkv = (xn @ w_qkv).reshape(b, s, 3, N_HEADS, HEAD_DIM)
    q, k, v = [jnp.transpose(qkv[:, :, i], (0, 2, 1, 3)) for i in range(3)]
    q = apply_partial_rope(q, cos, sin)
    k = apply_partial_rope(k, cos, sin)

    n_blocks = -(-s // MOBA_BLOCK)
    top_k = min(MOBA_TOPK, n_blocks)
    pad = n_blocks * MOBA_BLOCK - s
    kp = jnp.pad(k, ((0, 0), (0, 0), (0, pad), (0, 0)))
    vp = jnp.pad(v, ((0, 0), (0, 0), (0, pad), (0, 0)))
    kb = kp.reshape(b, N_HEADS, n_blocks, MOBA_BLOCK, HEAD_DIM)
    vb = vp.reshape(b, N_HEADS, n_blocks, MOBA_BLOCK, HEAD_DIM)
    k_mean = jnp.mean(kb.astype(jnp.float32), axis=3)

    scale = 1.0 / math.sqrt(HEAD_DIM)
    bi = jnp.arange(b)[:, None, None, None]
    hi = jnp.arange(N_HEADS)[None, :, None, None]
    blk_ids = jnp.arange(n_blocks)

    def chunk(c):
        start = c * Q_CHUNK
        qc = lax.dynamic_slice_in_dim(q, start, Q_CHUNK, axis=2)
        qpos = start + jnp.arange(Q_CHUNK)
        qblk = start // MOBA_BLOCK
        gate = jnp.einsum('bhqd,bhnd->bhqn', qc.astype(jnp.float32), k_mean)
        gate = jnp.where(blk_ids < qblk, gate, NEG_INF)
        _, sel = lax.top_k(gate, top_k)
        sel_valid = sel < qblk
        k_sel = kb[bi, hi, sel]
        v_sel = vb[bi, hi, sel]
        s_sel = jnp.einsum('bhqd,bhqnkd->bhqnk', qc, k_sel).astype(jnp.float32) * scale
        s_sel = jnp.where(sel_valid[..., None], s_sel, NEG_INF)
        s_sel = s_sel.reshape(b, N_HEADS, Q_CHUNK, top_k * MOBA_BLOCK)
        k_own = lax.dynamic_slice_in_dim(kp, qblk * MOBA_BLOCK, MOBA_BLOCK, axis=2)
        v_own = lax.dynamic_slice_in_dim(vp, qblk * MOBA_BLOCK, MOBA_BLOCK, axis=2)
        s_own = jnp.einsum('bhqd,bhkd->bhqk', qc, k_own).astype(jnp.float32) * scale
        kpos = qblk * MOBA_BLOCK + jnp.arange(MOBA_BLOCK)
        s_own = jnp.where(kpos[None, :] <= qpos[:, None], s_own, NEG_INF)
        p = jax.nn.softmax(jnp.concatenate([s_sel, s_own], axis=-1), axis=-1)
        p_sel = p[..., :top_k * MOBA_BLOCK].reshape(
            b, N_HEADS, Q_CHUNK, top_k, MOBA_BLOCK).astype(v.dtype)
        p_own = p[..., top_k * MOBA_BLOCK:].astype(v.dtype)
        return (jnp.einsum('bhqnk,bhqnkd->bhqd', p_sel, v_sel)
                + jnp.einsum('bhqk,bhkd->bhqd', p_own, v_own))

    outs = lax.map(chunk, jnp.arange(s // Q_CHUNK))
    out = jnp.transpose(outs, (1, 0, 3, 2, 4)).reshape(b, s, D_MODEL)
    return out @ w_o


def pool_mixer(xn, w_groups, ls_scale):
    b, s, _ = xn.shape
    xf = xn.astype(jnp.float32).reshape(b, s, N_POOL_GROUPS, POOL_GROUP_DIM)
    cs = jnp.concatenate([jnp.zeros((b, 1, N_POOL_GROUPS, POOL_GROUP_DIM), jnp.float32),
                          jnp.cumsum(xf, axis=1)], axis=1)
    t = jnp.arange(s)
    pooled = []
    for g, w in enumerate(POOL_WINDOWS):
        c = cs[:, :, g]
        lower = jnp.pad(c[:, :s - w + 1], ((0, 0), (w - 1, 0), (0, 0)))
        count = jnp.minimum(t + 1, w).astype(jnp.float32)[None, :, None]
        pooled.append((c[:, 1:] - lower) / count)
    pooled = jnp.stack(pooled, axis=2) - xf
    y = jnp.einsum('bsgc,gcd->bsgd', pooled.astype(xn.dtype), w_groups)
    return y.reshape(b, s, D_MODEL) * ls_scale


def short_conv_mixer(xn, w_in, conv_w, w_out):
    u = xn @ w_in
    gate_b, gate_c, h = jnp.split(u, 3, axis=-1)
    z = gate_c * h
    zc = lax.conv_general_dilated(
        z, conv_w[:, None, :].astype(z.dtype), window_strides=(1,),
        padding=[(CONV_WIDTH - 1, 0)], dimension_numbers=('NWC', 'WIO', 'NWC'),
        feature_group_count=D_MODEL)
    return (gate_b * zc) @ w_out


def squared_relu_mlp(x, w_up, w_down):
    h = jax.nn.relu(x @ w_up)
    return (h * h) @ w_down


def setup_inputs(seed: int = 0) -> dict:
    key = jax.random.key(seed)
    ks = jax.random.split(key, 16)
    n_attn = len(range(0, DEPTH, N_MIXERS))
    n_pool = len(range(1, DEPTH, N_MIXERS))
    n_conv = len(range(2, DEPTH, N_MIXERS))
    f32 = jnp.float32

    def nrm(k, shape, fan_in):
        return jax.random.normal(k, shape, f32) * (fan_in ** -0.5)

    x = jax.random.normal(ks[0], (BATCH, SEQ, D_MODEL), f32)
    positions = jnp.broadcast_to(jnp.arange(SEQ, dtype=jnp.int32), (BATCH, SEQ))
    norm_mix = 1.0 + 0.02 * jax.random.normal(ks[1], (DEPTH, D_MODEL), f32)
    norm_mlp = 1.0 + 0.02 * jax.random.normal(ks[2], (DEPTH, D_MODEL), f32)
    attn_w_qkv = nrm(ks[3], (n_attn, D_MODEL, 3 * D_MODEL), D_MODEL)
    attn_w_o = nrm(ks[4], (n_attn, D_MODEL, D_MODEL), D_MODEL)
    pool_w = nrm(ks[5], (n_pool, N_POOL_GROUPS, POOL_GROUP_DIM, POOL_GROUP_DIM), POOL_GROUP_DIM)
    pool_scale = 1.0 + 0.1 * jax.random.normal(ks[6], (n_pool, D_MODEL), f32)
    conv_w_in = nrm(ks[7], (n_conv, D_MODEL, 3 * D_MODEL), D_MODEL)
    conv_w = nrm(ks[8], (n_conv, CONV_WIDTH, D_MODEL), CONV_WIDTH)
    conv_w_out = nrm(ks[9], (n_conv, D_MODEL, D_MODEL), D_MODEL)
    mlp_w_up = nrm(ks[10], (DEPTH, D_MODEL, D_FF), D_MODEL)
    mlp_w_down = nrm(ks[11], (DEPTH, D_FF, D_MODEL), D_FF)
    norm_final = 1.0 + 0.02 * jax.random.normal(ks[12], (D_MODEL,), f32)
    return {"x": x, "positions": positions, "norm_mix": norm_mix, "norm_mlp": norm_mlp,
            "attn_w_qkv": attn_w_qkv, "attn_w_o": attn_w_o,
            "pool_w": pool_w, "pool_scale": pool_scale,
            "conv_w_in": conv_w_in, "conv_w": conv_w, "conv_w_out": conv_w_out,
            "mlp_w_up": mlp_w_up, "mlp_w_down": mlp_w_down, "norm_final": norm_final}


def reference(x, positions, norm_mix, norm_mlp, attn_w_qkv, attn_w_o, pool_w, pool_scale,
              conv_w_in, conv_w, conv_w_out, mlp_w_up, mlp_w_down, norm_final):
    cos, sin = rope_tables(positions)
    i_attn = i_pool = i_conv = 0
    for i in range(DEPTH):
        xn = rms_norm(x, norm_mix[i])
        kind = i % N_MIXERS
        if kind == 0:
            y = moba_attention(xn, attn_w_qkv[i_attn], attn_w_o[i_attn], cos, sin)
            i_attn += 1
        elif kind == 1:
            y = pool_mixer(xn, pool_w[i_pool], pool_scale[i_pool])
            i_pool += 1
        else:
            y = short_conv_mixer(xn, conv_w_in[i_conv], conv_w[i_conv], conv_w_out[i_conv])
            i_conv += 1
        x = x + y
        x = x + squared_relu_mlp(rms_norm(x, norm_mlp[i]), mlp_w_up[i], mlp_w_down[i])
    return rms_norm(x, norm_final)
```

```python
import functools
import math

import jax
import jax.numpy as jnp
from jax import lax
from jax.experimental import pallas as pl
from jax.experimental.pallas import tpu as pltpu

D_MODEL = 1024
N_HEADS = 8
HEAD_DIM = D_MODEL // N_HEADS
ROT_DIM = HEAD_DIM // 4
ROPE_THETA = 500000.0
MOBA_BLOCK = 256
MOBA_TOPK = 3
POOL_WINDOWS = (2, 4, 8, 16)
POOL_GROUP_DIM = D_MODEL // len(POOL_WINDOWS)
CONV_WIDTH = 3
D_FF = 4 * D_MODEL
NORM_EPS = 1e-6
NEG_INF = -1e30

HALO = 16
ROW_TILE = 512
FF_CHUNK = 1024
HEADS_PER_STEP = 4
VMEM_LIMIT = 56 * 1024 * 1024
Q_SCALE = (1.0 / math.sqrt(HEAD_DIM)) * math.log2(math.e)

F32 = jnp.float32
BF16 = jnp.bfloat16


def _resident(shape):
    nd = len(shape)
    return pl.BlockSpec(shape, lambda *_: (0,) * nd, pipeline_mode=pl.Buffered(1))


def _rms(x, g):
    ms = jnp.mean(x * x, axis=-1, keepdims=True)
    return x * lax.rsqrt(ms + NORM_EPS) * g


def _mlp_residual(x1, g_ref, wup_ref, wdn_ref):
    xn = _rms(x1, g_ref[...]).astype(BF16)
    acc = jnp.zeros_like(x1)
    for c in range(D_FF // FF_CHUNK):
        cs = slice(c * FF_CHUNK, (c + 1) * FF_CHUNK)
        h = jnp.dot(xn, wup_ref[:, cs], preferred_element_type=F32)
        h = jnp.maximum(h, 0.0)
        h = (h * h).astype(BF16)
        acc = acc + jnp.dot(h, wdn_ref[cs, :], preferred_element_type=F32)
    return x1 + acc


def _qkv_kernel(x_ref, g_ref, w_ref, cos_ref, sin_ref, qt_ref, k_ref, vt_ref, km_ref):
    tm = x_ref.shape[0]
    xn = _rms(x_ref[...], g_ref[...]).astype(BF16)
    cos = cos_ref[...]
    sin = sin_ref[...]
    lane = lax.broadcasted_iota(jnp.int32, (tm, HEAD_DIM), 1)
    first_half = lane < (ROT_DIM // 2)

    def rope(t):
        parts = []
        for h in range(N_HEADS):
            th = t[:, h * HEAD_DIM:(h + 1) * HEAD_DIM]
            partner = jnp.where(first_half,
                                pltpu.roll(th, HEAD_DIM - ROT_DIM // 2, 1),
                                pltpu.roll(th, ROT_DIM // 2, 1))
            parts.append(th * cos + partner * sin)
        return jnp.concatenate(parts, axis=1)

    q = jnp.dot(xn, w_ref[:, 0:D_MODEL], preferred_element_type=F32)
    q = rope(q) * Q_SCALE
    for b in range(tm // MOBA_BLOCK):
        qt_ref[b] = q[b * MOBA_BLOCK:(b + 1) * MOBA_BLOCK, :].T.astype(BF16)

    k = jnp.dot(xn, w_ref[:, D_MODEL:2 * D_MODEL], preferred_element_type=F32)
    k = rope(k)
    k_ref[...] = k.astype(BF16)
    for b in range(tm // MOBA_BLOCK):
        km_ref[b] = jnp.mean(k[b * MOBA_BLOCK:(b + 1) * MOBA_BLOCK, :], axis=0, keepdims=True)

    v = jnp.dot(xn, w_ref[:, 2 * D_MODEL:3 * D_MODEL], preferred_element_type=F32)
    for b in range(tm // MOBA_BLOCK):
        vt_ref[b] = v[b * MOBA_BLOCK:(b + 1) * MOBA_BLOCK, :].T.astype(BF16)


def _qkv_call(x, g, w, cos, sin):
    t = x.shape[0]
    tm = ROW_TILE
    nb = tm // MOBA_BLOCK
    return pl.pallas_call(
        _qkv_kernel,
        grid=(t // tm,),
        in_specs=[
            pl.BlockSpec((tm, D_MODEL), lambda i: (i, 0)),
            _resident((1, D_MODEL)),
            _resident((D_MODEL, 3 * D_MODEL)),
            pl.BlockSpec((tm, HEAD_DIM), lambda i: (i, 0)),
            pl.BlockSpec((tm, HEAD_DIM), lambda i: (i, 0)),
        ],
        out_specs=[
            pl.BlockSpec((nb, D_MODEL, MOBA_BLOCK), lambda i: (i, 0, 0)),
            pl.BlockSpec((tm, D_MODEL), lambda i: (i, 0)),
            pl.BlockSpec((nb, D_MODEL, MOBA_BLOCK), lambda i: (i, 0, 0)),
            pl.BlockSpec((nb, 1, D_MODEL), lambda i: (i, 0, 0)),
        ],
        out_shape=[
            jax.ShapeDtypeStruct((t // MOBA_BLOCK, D_MODEL, MOBA_BLOCK), BF16),
            jax.ShapeDtypeStruct((t, D_MODEL), BF16),
            jax.ShapeDtypeStruct((t // MOBA_BLOCK, D_MODEL, MOBA_BLOCK), BF16),
            jax.ShapeDtypeStruct((t // MOBA_BLOCK, 1, D_MODEL), F32),
        ],
        compiler_params=pltpu.CompilerParams(
            dimension_semantics=("arbitrary",), vmem_limit_bytes=VMEM_LIMIT),
        name="moba_qkv",
    )(x, g, w, cos, sin)


def _attn_kernel(qt_ref, k_ref, vt_ref, km_ref, o_ref, acc_ref, bias_ref, *, n_blocks):
    i = pl.program_id(2)
    heads = acc_ref.shape[0]
    blk = MOBA_BLOCK

    def hs(h):
        return slice(h * HEAD_DIM, (h + 1) * HEAD_DIM)

    row = lax.broadcasted_iota(jnp.int32, (n_blocks, blk), 0)
    valid = row < i
    for h in range(heads):
        gate = jnp.dot(km_ref[:, hs(h)].astype(BF16), qt_ref[0, hs(h), :],
                       preferred_element_type=F32)
        gate = jnp.where(valid, gate, NEG_INF)
        rank = jnp.zeros((n_blocks, blk), jnp.int32)
        for jp in range(n_blocks):
            other = gate[jp:jp + 1, :]
            wins_tie = jnp.where(other >= gate, 1, 0)
            wins = jnp.where(other > gate, 1, 0)
            rank = rank + jnp.where(row > jp, wins_tie, wins)
        chosen = jnp.logical_and(rank < MOBA_TOPK, valid)
        bias_ref[h] = jnp.where(chosen, 0.0, NEG_INF)

    kpos = lax.broadcasted_iota(jnp.int32, (blk, blk), 0)
    qpos = lax.broadcasted_iota(jnp.int32, (blk, blk), 1)
    causal = kpos <= qpos
    ms, ls = [], []
    for h in range(heads):
        k_own = k_ref[pl.ds(pl.multiple_of(i * blk, blk), blk), hs(h)]
        s = jnp.dot(k_own, qt_ref[0, hs(h), :], preferred_element_type=F32)
        s = jnp.where(causal, s, NEG_INF)
        m = jnp.max(s, axis=0, keepdims=True)
        p = jnp.exp2(s - m)
        ls.append(jnp.sum(p, axis=0, keepdims=True))
        ms.append(m)
        acc_ref[h] = jnp.dot(vt_ref[i, hs(h), :], p.astype(BF16), preferred_element_type=F32)

    def past_block(j, carry):
        ms, ls = carry
        new_ms, new_ls = [], []
        for h in range(heads):
            k_j = k_ref[pl.ds(pl.multiple_of(j * blk, blk), blk), hs(h)]
            s = jnp.dot(k_j, qt_ref[0, hs(h), :], preferred_element_type=F32)
            s = s + bias_ref[h, pl.ds(j, 1), :]
            m_new = jnp.maximum(ms[h], jnp.max(s, axis=0, keepdims=True))
            alpha = jnp.exp2(ms[h] - m_new)
            p = jnp.exp2(s - m_new)
            new_ls.append(alpha * ls[h] + jnp.sum(p, axis=0, keepdims=True))
            new_ms.append(m_new)
            pv = jnp.dot(vt_ref[j, hs(h), :], p.astype(BF16), preferred_element_type=F32)
            acc_ref[h] = alpha * acc_ref[h] + pv
        return tuple(new_ms), tuple(new_ls)

    ms, ls = lax.fori_loop(0, i, past_block, (tuple(ms), tuple(ls)))

    for h in range(heads):
        out_t = acc_ref[h] * (1.0 / ls[h])
        o_ref[:, hs(h)] = out_t.T.astype(BF16)


def _attn_call(qt, k, vt, km, batch, seq):
    n_blocks = seq // MOBA_BLOCK
    u = HEADS_PER_STEP
    uw = u * HEAD_DIM
    return pl.pallas_call(
        functools.partial(_attn_kernel, n_blocks=n_blocks),
        grid=(batch, N_HEADS // u, n_blocks),
        in_specs=[
            pl.BlockSpec((1, uw, MOBA_BLOCK), lambda b, g, i: (b * n_blocks + i, g, 0)),
            pl.BlockSpec((seq, uw), lambda b, g, i: (b, g)),
            pl.BlockSpec((n_blocks, uw, MOBA_BLOCK), lambda b, g, i: (b, g, 0)),
            pl.BlockSpec((n_blocks, uw), lambda b, g, i: (b, g)),
        ],
        out_specs=pl.BlockSpec((MOBA_BLOCK, uw), lambda b, g, i: (b * n_blocks + i, g)),
        out_shape=jax.ShapeDtypeStruct((batch * seq, D_MODEL), BF16),
        scratch_shapes=[
            pltpu.VMEM((u, HEAD_DIM, MOBA_BLOCK), F32),
            pltpu.VMEM((u, n_blocks, MOBA_BLOCK), F32),
        ],
        compiler_params=pltpu.CompilerParams(
            dimension_semantics=("arbitrary", "arbitrary", "arbitrary"),
            vmem_limit_bytes=VMEM_LIMIT),
        name="moba_attention",
    )(qt, k, vt, km)


def _attn_mlp_kernel(x_ref, a_ref, wo_ref, g_ref, wup_ref, wdn_ref, *rest, final):
    if final:
        gf_ref, o_ref = rest
    else:
        (o_ref,) = rest
    x1 = x_ref[...] + jnp.dot(a_ref[...], wo_ref[...], preferred_element_type=F32)
    out = _mlp_residual(x1, g_ref, wup_ref, wdn_ref)
    if final:
        out = _rms(out, gf_ref[...])
    o_ref[...] = out


def _attn_mlp_call(x, a, wo, g, wup, wdn, g_final=None):
    t = x.shape[0]
    tm = ROW_TILE
    final = g_final is not None
    in_specs = [
        pl.BlockSpec((tm, D_MODEL), lambda i: (i, 0)),
        pl.BlockSpec((tm, D_MODEL), lambda i: (i, 0)),
        _resident((D_MODEL, D_MODEL)),
        _resident((1, D_MODEL)),
        _resident((D_MODEL, D_FF)),
        _resident((D_FF, D_MODEL)),
    ]
    args = [x, a, wo, g, wup, wdn]
    if final:
        in_specs.append(_resident((1, D_MODEL)))
        args.append(g_final)
    return pl.pallas_call(
        functools.partial(_attn_mlp_kernel, final=final),
        grid=(t // tm,),
        in_specs=in_specs,
        out_specs=pl.BlockSpec((tm, D_MODEL), lambda i: (i, 0)),
        out_shape=jax.ShapeDtypeStruct((t, D_MODEL), F32),
        compiler_params=pltpu.CompilerParams(
            dimension_semantics=("arbitrary",), vmem_limit_bytes=VMEM_LIMIT),
        name="attn_out_mlp",
    )(*args)


def _halo_spec(tm):
    per = tm // HALO
    return pl.BlockSpec((HALO, D_MODEL), lambda i: (jnp.maximum(i * per - 1, 0), 0))


def _pool_mlp_kernel(x_ref, halo_ref, gmix_ref, pw_ref, ps_ref, g_ref, wup_ref, wdn_ref,
                     o_ref, ext_ref, *, tiles_per_seq):
    tm = x_ref.shape[0]
    tile_in_seq = pl.program_id(0) % tiles_per_seq
    x = x_ref[...]
    gmix = gmix_ref[...]
    xn = _rms(x, gmix)
    ext_ref[0:HALO, :] = jnp.where(tile_in_seq == 0, 0.0, _rms(halo_ref[...], gmix))
    ext_ref[HALO:HALO + tm, :] = xn
    pos = tile_in_seq * tm + lax.broadcasted_iota(jnp.int32, (tm, 1), 0)
    ys = []
    for g, w in enumerate(POOL_WINDOWS):
        cs = slice(g * POOL_GROUP_DIM, (g + 1) * POOL_GROUP_DIM)
        own = xn[:, cs]
        total = own
        for back in range(1, w):
            total = total + ext_ref[HALO - back:HALO - back + tm, cs]
        count = jnp.minimum(pos + 1, w).astype(F32)
        pooled = total / count - own
        ys.append(jnp.dot(pooled.astype(BF16), pw_ref[g], preferred_element_type=F32))
    x1 = x + jnp.concatenate(ys, axis=1) * ps_ref[...]
    o_ref[...] = _mlp_residual(x1, g_ref, wup_ref, wdn_ref)


def _pool_mlp_call(x, gmix, pw, ps, g, wup, wdn, seq):
    t = x.shape[0]
    tm = ROW_TILE
    n_groups = len(POOL_WINDOWS)
    return pl.pallas_call(
        functools.partial(_pool_mlp_kernel, tiles_per_seq=seq // tm),
        grid=(t // tm,),
        in_specs=[
            pl.BlockSpec((tm, D_MODEL), lambda i: (i, 0)),
            _halo_spec(tm),
            _resident((1, D_MODEL)),
            _resident((n_groups, POOL_GROUP_DIM, POOL_GROUP_DIM)),
            _resident((1, D_MODEL)),
            _resident((1, D_MODEL)),
            _resident((D_MODEL, D_FF)),
            _resident((D_FF, D_MODEL)),
        ],
        out_specs=pl.BlockSpec((tm, D_MODEL), lambda i: (i, 0)),
        out_shape=jax.ShapeDtypeStruct((t, D_MODEL), F32),
        scratch_shapes=[pltpu.VMEM((HALO + tm, D_MODEL), F32)],
        compiler_params=pltpu.CompilerParams(
            dimension_semantics=("arbitrary",), vmem_limit_bytes=VMEM_LIMIT),
        name="pool_mlp",
    )(x, x, gmix, pw, ps, g, wup, wdn)


def _conv_mlp_kernel(x_ref, halo_ref, gmix_ref, win_ref, cw_ref, wout_ref, g_ref, wup_ref,
                     wdn_ref, o_ref, xe_ref, z_ref, *, tiles_per_seq):
    tm = x_ref.shape[0]
    tile_in_seq = pl.program_id(0) % tiles_per_seq
    x = x_ref[...]
    gmix = gmix_ref[...]
    xe_ref[0:HALO, :] = _rms(halo_ref[...], gmix).astype(BF16)
    xe_ref[HALO:HALO + tm, :] = _rms(x, gmix).astype(BF16)
    gate_c = jnp.dot(xe_ref[...], win_ref[:, D_MODEL:2 * D_MODEL], preferred_element_type=F32)
    hid = jnp.dot(xe_ref[...], win_ref[:, 2 * D_MODEL:3 * D_MODEL], preferred_element_type=F32)
    z_ref[...] = gate_c * hid

    @pl.when(tile_in_seq == 0)
    def _():
        z_ref[0:HALO, :] = jnp.zeros((HALO, D_MODEL), F32)

    cw = cw_ref[...]
    zc = cw[CONV_WIDTH - 1:CONV_WIDTH, :] * z_ref[HALO:HALO + tm, :]
    for back in range(1, CONV_WIDTH):
        tap = cw[CONV_WIDTH - 1 - back:CONV_WIDTH - back, :]
        zc = zc + tap * z_ref[HALO - back:HALO - back + tm, :]
    gate_b = jnp.dot(xe_ref[HALO:HALO + tm, :], win_ref[:, 0:D_MODEL],
                     preferred_element_type=F32)
    y = jnp.dot((gate_b * zc).astype(BF16), wout_ref[...], preferred_element_type=F32)
    o_ref[...] = _mlp_residual(x + y, g_ref, wup_ref, wdn_ref)


def _conv_mlp_call(x, gmix, win, cw, wout, g, wup, wdn, seq):
    t = x.shape[0]
    tm = ROW_TILE
    return pl.pallas_call(
        functools.partial(_conv_mlp_kernel, tiles_per_seq=seq // tm),
        grid=(t // tm,),
        in_specs=[
            pl.BlockSpec((tm, D_MODEL), lambda i: (i, 0)),
            _halo_spec(tm),
            _resident((1, D_MODEL)),
            _resident((D_MODEL, 3 * D_MODEL)),
            _resident((CONV_WIDTH, D_MODEL)),
            _resident((D_MODEL, D_MODEL)),
            _resident((1, D_MODEL)),
            _resident((D_MODEL, D_FF)),
            _resident((D_FF, D_MODEL)),
        ],
        out_specs=pl.BlockSpec((tm, D_MODEL), lambda i: (i, 0)),
        out_shape=jax.ShapeDtypeStruct((t, D_MODEL), F32),
        scratch_shapes=[
            pltpu.VMEM((HALO + tm, D_MODEL), BF16),
            pltpu.VMEM((HALO + tm, D_MODEL), F32),
        ],
        compiler_params=pltpu.CompilerParams(
            dimension_semantics=("arbitrary",), vmem_limit_bytes=VMEM_LIMIT),
        name="conv_mlp",
    )(x, x, gmix, win, cw, wout, g, wup, wdn)


def _rope_tables(positions):
    inv_freq = ROPE_THETA ** (-jnp.arange(0, ROT_DIM, 2, dtype=F32) / ROT_DIM)
    ang = positions.astype(F32).reshape(-1, 1) * inv_freq
    c, s = jnp.cos(ang), jnp.sin(ang)
    rest = HEAD_DIM - ROT_DIM
    n = ang.shape[0]
    cos = jnp.concatenate([c, c, jnp.ones((n, rest), F32)], axis=1)
    sin = jnp.concatenate([-s, s, jnp.zeros((n, rest), F32)], axis=1)
    return cos, sin


def kernel(x, positions, norm_mix, norm_mlp, attn_w_qkv, attn_w_o, pool_w, pool_scale,
           conv_w_in, conv_w, conv_w_out, mlp_w_up, mlp_w_down, norm_final):
    batch, seq, d = x.shape
    depth = norm_mix.shape[0]
    assert d == D_MODEL and seq % ROW_TILE == 0 and ROW_TILE % MOBA_BLOCK == 0
    cos, sin = _rope_tables(positions)
    h = x.reshape(batch * seq, d)
    row = lambda v: v.reshape(1, -1)
    i_attn = i_pool = i_conv = 0
    for i in range(depth):
        kind = i % 3
        g_mlp = row(norm_mlp[i])
        wup = mlp_w_up[i].astype(BF16)
        wdn = mlp_w_down[i].astype(BF16)
        g_final = row(norm_final) if i == depth - 1 else None
        if kind == 0:
            qt, k, vt, km = _qkv_call(h, row(norm_mix[i]), attn_w_qkv[i_attn].astype(BF16),
                                      cos, sin)
            a = _attn_call(qt, k, vt, km.reshape(-1, d), batch, seq)
            h = _attn_mlp_call(h, a, attn_w_o[i_attn].astype(BF16), g_mlp, wup, wdn, g_final)
            i_attn += 1
        elif kind == 1:
            h = _pool_mlp_call(h, row(norm_mix[i]), pool_w[i_pool].astype(BF16),
                               row(pool_scale[i_pool]), g_mlp, wup, wdn, seq)
            i_pool += 1
        else:
            h = _conv_mlp_call(h, row(norm_mix[i]), conv_w_in[i_conv].astype(BF16),
                               conv_w[i_conv], conv_w_out[i_conv].astype(BF16),
                               g_mlp, wup, wdn, seq)
            i_conv += 1
        if g_final is not None and kind != 0:
            raise NotImplementedError("final norm is fused only into the MoBA layer's MLP stage")
    return h.reshape(batch, seq, d)
```

```python
import functools
import math

import jax
import jax.numpy as jnp
from jax import lax
from jax.experimental import pallas as pl
from jax.experimental.pallas import tpu as pltpu

D_MODEL = 1024
N_HEADS = 8
HEAD_DIM = D_MODEL // N_HEADS
ROT_DIM = HEAD_DIM // 4
ROPE_THETA = 500000.0
MOBA_BLOCK = 256
MOBA_TOPK = 3
POOL_WINDOWS = (2, 4, 8, 16)
POOL_GROUP_DIM = D_MODEL // len(POOL_WINDOWS)
CONV_WIDTH = 3
D_FF = 4 * D_MODEL
NORM_EPS = 1e-6
NEG_INF = -1e30

HALO = 16
ROW_TILE = 512
FF_CHUNK = 1024
VMEM_LIMIT = 56 * 1024 * 1024
Q_SCALE = (1.0 / math.sqrt(HEAD_DIM)) * math.log2(math.e)

F32 = jnp.float32
BF16 = jnp.bfloat16


def _resident(shape):
    nd = len(shape)
    return pl.BlockSpec(shape, lambda *_: (0,) * nd, pipeline_mode=pl.Buffered(1))


def _rms(x, g):
    ms = jnp.mean(x * x, axis=-1, keepdims=True)
    return x * lax.rsqrt(ms + NORM_EPS) * g


def _mlp_residual(x1, g_ref, wup_ref, wdn_ref):
    xn = _rms(x1, g_ref[...]).astype(BF16)
    acc = jnp.zeros_like(x1)
    for c in range(D_FF // FF_CHUNK):
        cs = slice(c * FF_CHUNK, (c + 1) * FF_CHUNK)
        h = jnp.dot(xn, wup_ref[:, cs], preferred_element_type=F32)
        h = jnp.maximum(h, 0.0)
        h = (h * h).astype(BF16)
        acc = acc + jnp.dot(h, wdn_ref[cs, :], preferred_element_type=F32)
    return x1 + acc


def _qkv_kernel(x_ref, g_ref, w_ref, cos_ref, sin_ref, qt_ref, k_ref, vt_ref, km_ref):
    tm = x_ref.shape[0]
    xn = _rms(x_ref[...], g_ref[...]).astype(BF16)
    cos = cos_ref[...]
    sin = sin_ref[...]
    lane = lax.broadcasted_iota(jnp.int32, (tm, HEAD_DIM), 1)
    first_half = lane < (ROT_DIM // 2)

    def rope(t):
        parts = []
        for h in range(N_HEADS):
            th = t[:, h * HEAD_DIM:(h + 1) * HEAD_DIM]
            partner = jnp.where(first_half,
                                pltpu.roll(th, HEAD_DIM - ROT_DIM // 2, 1),
                                pltpu.roll(th, ROT_DIM // 2, 1))
            parts.append(th * cos + partner * sin)
        return jnp.concatenate(parts, axis=1)

    q = jnp.dot(xn, w_ref[:, 0:D_MODEL], preferred_element_type=F32)
    q = rope(q) * Q_SCALE
    for b in range(tm // MOBA_BLOCK):
        qt_ref[b] = q[b * MOBA_BLOCK:(b + 1) * MOBA_BLOCK, :].T.astype(BF16)

    k = jnp.dot(xn, w_ref[:, D_MODEL:2 * D_MODEL], preferred_element_type=F32)
    k = rope(k)
    k_ref[...] = k.astype(BF16)
    for b in range(tm // MOBA_BLOCK):
        km_ref[b] = jnp.mean(k[b * MOBA_BLOCK:(b + 1) * MOBA_BLOCK, :], axis=0, keepdims=True)

    v = jnp.dot(xn, w_ref[:, 2 * D_MODEL:3 * D_MODEL], preferred_element_type=F32)
    for b in range(tm // MOBA_BLOCK):
        vt_ref[b] = v[b * MOBA_BLOCK:(b + 1) * MOBA_BLOCK, :].T.astype(BF16)


def _qkv_call(x, g, w, cos, sin):
    t = x.shape[0]
    tm = ROW_TILE
    nb = tm // MOBA_BLOCK
    return pl.pallas_call(
        _qkv_kernel,
        grid=(t // tm,),
        in_specs=[
            pl.BlockSpec((tm, D_MODEL), lambda i: (i, 0)),
            _resident((1, D_MODEL)),
            _resident((D_MODEL, 3 * D_MODEL)),
            pl.BlockSpec((tm, HEAD_DIM), lambda i: (i, 0)),
            pl.BlockSpec((tm, HEAD_DIM), lambda i: (i, 0)),
        ],
        out_specs=[
            pl.BlockSpec((nb, D_MODEL, MOBA_BLOCK), lambda i: (i, 0, 0)),
            pl.BlockSpec((tm, D_MODEL), lambda i: (i, 0)),
            pl.BlockSpec((nb, D_MODEL, MOBA_BLOCK), lambda i: (i, 0, 0)),
            pl.BlockSpec((nb, 1, D_MODEL), lambda i: (i, 0, 0)),
        ],
        out_shape=[
            jax.ShapeDtypeStruct((t // MOBA_BLOCK, D_MODEL, MOBA_BLOCK), BF16),
            jax.ShapeDtypeStruct((t, D_MODEL), BF16),
            jax.ShapeDtypeStruct((t // MOBA_BLOCK, D_MODEL, MOBA_BLOCK), BF16),
            jax.ShapeDtypeStruct((t // MOBA_BLOCK, 1, D_MODEL), F32),
        ],
        compiler_params=pltpu.CompilerParams(
            dimension_semantics=("arbitrary",), vmem_limit_bytes=VMEM_LIMIT),
        name="moba_qkv",
    )(x, g, w, cos, sin)


def _attn_kernel(qt_ref, k_ref, vt_ref, km_ref, o_ref, s_ref, acc_ref, bias_ref, m_ref, l_ref, *,
                 n_blocks):
    i = pl.program_id(1)
    heads = acc_ref.shape[0]
    blk = MOBA_BLOCK

    def hs(h):
        return slice(h * HEAD_DIM, (h + 1) * HEAD_DIM)

    def scores(j, h):
        rows = pl.ds(pl.multiple_of(j * blk, blk), blk)
        return jnp.dot(k_ref[rows, hs(h)], qt_ref[0, hs(h), :], preferred_element_type=F32)

    row = lax.broadcasted_iota(jnp.int32, (n_blocks, blk), 0)
    valid = row < i
    for h in range(heads):
        gate = jnp.dot(km_ref[:, hs(h)].astype(BF16), qt_ref[0, hs(h), :],
                       preferred_element_type=F32)
        gate = jnp.where(valid, gate, NEG_INF)
        rank = jnp.zeros((n_blocks, blk), jnp.int32)
        for jp in range(n_blocks):
            other = gate[jp:jp + 1, :]
            wins_tie = jnp.where(other >= gate, 1, 0)
            wins = jnp.where(other > gate, 1, 0)
            rank = rank + jnp.where(row > jp, wins_tie, wins)
        chosen = jnp.logical_and(rank < MOBA_TOPK, valid)
        bias_ref[h] = jnp.where(chosen, 0.0, NEG_INF)

    kpos = lax.broadcasted_iota(jnp.int32, (blk, blk), 0)
    qpos = lax.broadcasted_iota(jnp.int32, (blk, blk), 1)
    causal = kpos <= qpos
    own = []
    for h in range(heads):
        own.append(scores(i, h))
        s_ref[0, h] = scores(0, h)
    ms, ls = [], []
    for h in range(heads):
        s = jnp.where(causal, own[h], NEG_INF)
        m = jnp.max(s, axis=0, keepdims=True)
        p = jnp.exp2(s - m)
        ls.append(jnp.sum(p, axis=0, keepdims=True))
        ms.append(m)
        acc_ref[h] = jnp.dot(vt_ref[i, hs(h), :], p.astype(BF16), preferred_element_type=F32)

    def past_block(h, slot, j, m_old, l_old):
        s = s_ref[slot, h]
        bias = bias_ref[h, pl.ds(j, 1), :]
        m_new = jnp.maximum(m_old, jnp.max(s, axis=0, keepdims=True) + bias)
        alpha = jnp.exp2(m_old - m_new)
        p = jnp.exp2(s - (m_new - bias))
        l_new = alpha * l_old + jnp.sum(p, axis=0, keepdims=True)
        pv = jnp.dot(vt_ref[j, hs(h), :], p.astype(BF16), preferred_element_type=F32)
        acc_ref[h] = alpha * acc_ref[h] + pv
        return m_new, l_new

    def block_pair(u, carry):
        m_all, l_all = carry
        j0 = 2 * u
        j_next = jnp.minimum(j0 + 2, i - 1)
        ms, ls = [], []
        for h in range(heads):
            m, l = past_block(h, 0, j0, m_all[h:h + 1, :], l_all[h:h + 1, :])
            s_ref[1, h] = scores(j0 + 1, h)
            ms.append(m)
            ls.append(l)
        for h in range(heads):
            ms[h], ls[h] = past_block(h, 1, j0 + 1, ms[h], ls[h])
            s_ref[0, h] = scores(j_next, h)
        return jnp.concatenate(ms, axis=0), jnp.concatenate(ls, axis=0)

    m_all, l_all = lax.fori_loop(
        0, lax.shift_right_logical(i, 1), block_pair,
        (jnp.concatenate(ms, axis=0), jnp.concatenate(ls, axis=0)))
    m_ref[...] = m_all
    l_ref[...] = l_all

    @pl.when((i & 1) == 1)
    def _():
        for h in range(heads):
            m, l = past_block(h, 0, i - 1, m_ref[h:h + 1, :], l_ref[h:h + 1, :])
            m_ref[h:h + 1, :] = m
            l_ref[h:h + 1, :] = l

    for h in range(heads):
        out_t = acc_ref[h] * (1.0 / l_ref[h:h + 1, :])
        o_ref[:, hs(h)] = out_t.T.astype(BF16)


def _attn_call(qt, k, vt, km, batch, seq):
    n_blocks = seq // MOBA_BLOCK
    return pl.pallas_call(
        functools.partial(_attn_kernel, n_blocks=n_blocks),
        grid=(batch, n_blocks),
        in_specs=[
            pl.BlockSpec((1, D_MODEL, MOBA_BLOCK), lambda b, i: (b * n_blocks + i, 0, 0)),
            pl.BlockSpec((seq, D_MODEL), lambda b, i: (b, 0)),
            pl.BlockSpec((n_blocks, D_MODEL, MOBA_BLOCK), lambda b, i: (b, 0, 0)),
            pl.BlockSpec((n_blocks, D_MODEL), lambda b, i: (b, 0)),
        ],
        out_specs=pl.BlockSpec((MOBA_BLOCK, D_MODEL), lambda b, i: (b * n_blocks + i, 0)),
        out_shape=jax.ShapeDtypeStruct((batch * seq, D_MODEL), BF16),
        scratch_shapes=[
            pltpu.VMEM((2, N_HEADS, MOBA_BLOCK, MOBA_BLOCK), F32),
            pltpu.VMEM((N_HEADS, HEAD_DIM, MOBA_BLOCK), F32),
            pltpu.VMEM((N_HEADS, n_blocks, MOBA_BLOCK), F32),
            pltpu.VMEM((N_HEADS, MOBA_BLOCK), F32),
            pltpu.VMEM((N_HEADS, MOBA_BLOCK), F32),
        ],
        compiler_params=pltpu.CompilerParams(
            dimension_semantics=("arbitrary", "arbitrary"), vmem_limit_bytes=VMEM_LIMIT),
        name="moba_attention",
    )(qt, k, vt, km)


def _attn_mlp_kernel(x_ref, a_ref, wo_ref, g_ref, wup_ref, wdn_ref, *rest, final):
    if final:
        gf_ref, o_ref = rest
    else:
        (o_ref,) = rest
    x1 = x_ref[...] + jnp.dot(a_ref[...], wo_ref[...], preferred_element_type=F32)
    out = _mlp_residual(x1, g_ref, wup_ref, wdn_ref)
    if final:
        out = _rms(out, gf_ref[...])
    o_ref[...] = out


def _attn_mlp_call(x, a, wo, g, wup, wdn, g_final=None):
    t = x.shape[0]
    tm = ROW_TILE
    final = g_final is not None
    in_specs = [
        pl.BlockSpec((tm, D_MODEL), lambda i: (i, 0)),
        pl.BlockSpec((tm, D_MODEL), lambda i: (i, 0)),
        _resident((D_MODEL, D_MODEL)),
        _resident((1, D_MODEL)),
        _resident((D_MODEL, D_FF)),
        _resident((D_FF, D_MODEL)),
    ]
    args = [x, a, wo, g, wup, wdn]
    if final:
        in_specs.append(_resident((1, D_MODEL)))
        args.append(g_final)
    return pl.pallas_call(
        functools.partial(_attn_mlp_kernel, final=final),
        grid=(t // tm,),
        in_specs=in_specs,
        out_specs=pl.BlockSpec((tm, D_MODEL), lambda i: (i, 0)),
        out_shape=jax.ShapeDtypeStruct((t, D_MODEL), F32),
        compiler_params=pltpu.CompilerParams(
            dimension_semantics=("arbitrary",), vmem_limit_bytes=VMEM_LIMIT),
        name="attn_out_mlp",
    )(*args)


def _halo_spec(tm):
    per = tm // HALO
    return pl.BlockSpec((HALO, D_MODEL), lambda i: (jnp.maximum(i * per - 1, 0), 0))


def _pool_mlp_kernel(x_ref, halo_ref, gmix_ref, pw_ref, ps_ref, g_ref, wup_ref, wdn_ref,
                     o_ref, ext_ref, *, tiles_per_seq):
    tm = x_ref.shape[0]
    tile_in_seq = pl.program_id(0) % tiles_per_seq
    x = x_ref[...]
    gmix = gmix_ref[...]
    xn = _rms(x, gmix)
    ext_ref[0:HALO, :] = jnp.where(tile_in_seq == 0, 0.0, _rms(halo_ref[...], gmix))
    ext_ref[HALO:HALO + tm, :] = xn
    pos = tile_in_seq * tm + lax.broadcasted_iota(jnp.int32, (tm, 1), 0)
    ys = []
    for g, w in enumerate(POOL_WINDOWS):
        cs = slice(g * POOL_GROUP_DIM, (g + 1) * POOL_GROUP_DIM)
        own = xn[:, cs]
        total = own
        for back in range(1, w):
            total = total + ext_ref[HALO - back:HALO - back + tm, cs]
        count = jnp.minimum(pos + 1, w).astype(F32)
        pooled = total / count - own
        ys.append(jnp.dot(pooled.astype(BF16), pw_ref[g], preferred_element_type=F32))
    x1 = x + jnp.concatenate(ys, axis=1) * ps_ref[...]
    o_ref[...] = _mlp_residual(x1, g_ref, wup_ref, wdn_ref)


def _pool_mlp_call(x, gmix, pw, ps, g, wup, wdn, seq):
    t = x.shape[0]
    tm = ROW_TILE
    n_groups = len(POOL_WINDOWS)
    return pl.pallas_call(
        functools.partial(_pool_mlp_kernel, tiles_per_seq=seq // tm),
        grid=(t // tm,),
        in_specs=[
            pl.BlockSpec((tm, D_MODEL), lambda i: (i, 0)),
            _halo_spec(tm),
            _resident((1, D_MODEL)),
            _resident((n_groups, POOL_GROUP_DIM, POOL_GROUP_DIM)),
            _resident((1, D_MODEL)),
            _resident((1, D_MODEL)),
            _resident((D_MODEL, D_FF)),
            _resident((D_FF, D_MODEL)),
        ],
        out_specs=pl.BlockSpec((tm, D_MODEL), lambda i: (i, 0)),
        out_shape=jax.ShapeDtypeStruct((t, D_MODEL), F32),
        scratch_shapes=[pltpu.VMEM((HALO + tm, D_MODEL), F32)],
        compiler_params=pltpu.CompilerParams(
            dimension_semantics=("arbitrary",), vmem_limit_bytes=VMEM_LIMIT),
        name="pool_mlp",
    )(x, x, gmix, pw, ps, g, wup, wdn)


def _conv_mlp_kernel(x_ref, halo_ref, gmix_ref, win_ref, cw_ref, wout_ref, g_ref, wup_ref,
                     wdn_ref, o_ref, xe_ref, z_ref, *, tiles_per_seq):
    tm = x_ref.shape[0]
    tile_in_seq = pl.program_id(0) % tiles_per_seq
    x = x_ref[...]
    gmix = gmix_ref[...]
    xe_ref[0:HALO, :] = _rms(halo_ref[...], gmix).astype(BF16)
    xe_ref[HALO:HALO + tm, :] = _rms(x, gmix).astype(BF16)
    gate_c = jnp.dot(xe_ref[...], win_ref[:, D_MODEL:2 * D_MODEL], preferred_element_type=F32)
    hid = jnp.dot(xe_ref[...], win_ref[:, 2 * D_MODEL:3 * D_MODEL], preferred_element_type=F32)
    z_ref[...] = gate_c * hid

    @pl.when(tile_in_seq == 0)
    def _():
        z_ref[0:HALO, :] = jnp.zeros((HALO, D_MODEL), F32)

    cw = cw_ref[...]
    zc = cw[CONV_WIDTH - 1:CONV_WIDTH, :] * z_ref[HALO:HALO + tm, :]
    for back in range(1, CONV_WIDTH):
        tap = cw[CONV_WIDTH - 1 - back:CONV_WIDTH - back, :]
        zc = zc + tap * z_ref[HALO - back:HALO - back + tm, :]
    gate_b = jnp.dot(xe_ref[HALO:HALO + tm, :], win_ref[:, 0:D_MODEL],
                     preferred_element_type=F32)
    y = jnp.dot((gate_b * zc).astype(BF16), wout_ref[...], preferred_element_type=F32)
    o_ref[...] = _mlp_residual(x + y, g_ref, wup_ref, wdn_ref)


def _conv_mlp_call(x, gmix, win, cw, wout, g, wup, wdn, seq):
    t = x.shape[0]
    tm = ROW_TILE
    return pl.pallas_call(
        functools.partial(_conv_mlp_kernel, tiles_per_seq=seq // tm),
        grid=(t // tm,),
        in_specs=[
            pl.BlockSpec((tm, D_MODEL), lambda i: (i, 0)),
            _halo_spec(tm),
            _resident((1, D_MODEL)),
            _resident((D_MODEL, 3 * D_MODEL)),
            _resident((CONV_WIDTH, D_MODEL)),
            _resident((D_MODEL, D_MODEL)),
            _resident((1, D_MODEL)),
            _resident((D_MODEL, D_FF)),
            _resident((D_FF, D_MODEL)),
        ],
        out_specs=pl.BlockSpec((tm, D_MODEL), lambda i: (i, 0)),
        out_shape=jax.ShapeDtypeStruct((t, D_MODEL), F32),
        scratch_shapes=[
            pltpu.VMEM((HALO + tm, D_MODEL), BF16),
            pltpu.VMEM((HALO + tm, D_MODEL), F32),
        ],
        compiler_params=pltpu.CompilerParams(
            dimension_semantics=("arbitrary",), vmem_limit_bytes=VMEM_LIMIT),
        name="conv_mlp",
    )(x, x, gmix, win, cw, wout, g, wup, wdn)


def _rope_tables(positions):
    inv_freq = ROPE_THETA ** (-jnp.arange(0, ROT_DIM, 2, dtype=F32) / ROT_DIM)
    ang = positions.astype(F32).reshape(-1, 1) * inv_freq
    c, s = jnp.cos(ang), jnp.sin(ang)
    rest = HEAD_DIM - ROT_DIM
    n = ang.shape[0]
    cos = jnp.concatenate([c, c, jnp.ones((n, rest), F32)], axis=1)
    sin = jnp.concatenate([-s, s, jnp.zeros((n, rest), F32)], axis=1)
    return cos, sin


def kernel(x, positions, norm_mix, norm_mlp, attn_w_qkv, attn_w_o, pool_w, pool_scale,
           conv_w_in, conv_w, conv_w_out, mlp_w_up, mlp_w_down, norm_final):
    batch, seq, d = x.shape
    depth = norm_mix.shape[0]
    assert d == D_MODEL and seq % ROW_TILE == 0 and ROW_TILE % MOBA_BLOCK == 0
    assert (depth - 1) % 3 == 0, "the final norm is fused into a MoBA layer's MLP stage"
    cos, sin = _rope_tables(positions)
    h = x.reshape(batch * seq, d)
    row = lambda v: v.reshape(1, -1)
    i_attn = i_pool = i_conv = 0
    for i in range(depth):
        kind = i % 3
        g_mlp = row(norm_mlp[i])
        wup = mlp_w_up[i].astype(BF16)
        wdn = mlp_w_down[i].astype(BF16)
        if kind == 0:
            g_final = row(norm_final) if i == depth - 1 else None
            qt, k, vt, km = _qkv_call(h, row(norm_mix[i]), attn_w_qkv[i_attn].astype(BF16),
                                      cos, sin)
            a = _attn_call(qt, k, vt, km.reshape(-1, d), batch, seq)
            h = _attn_mlp_call(h, a, attn_w_o[i_attn].astype(BF16), g_mlp, wup, wdn, g_final)
            i_attn += 1
        elif kind == 1:
            h = _pool_mlp_call(h, row(norm_mix[i]), pool_w[i_pool].astype(BF16),
                               row(pool_scale[i_pool]), g_mlp, wup, wdn, seq)
            i_pool += 1
        else:
            h = _conv_mlp_call(h, row(norm_mix[i]), conv_w_in[i_conv].astype(BF16),
                               conv_w[i_conv], conv_w_out[i_conv].astype(BF16),
                               g_mlp, wup, wdn, seq)
            i_conv += 1
    return h.reshape(batch, seq, d)
```

```python
import functools
import math

import jax
import jax.numpy as jnp
from jax import lax
from jax.experimental import pallas as pl
from jax.experimental.pallas import tpu as pltpu

D_MODEL = 1024
N_HEADS = 8
HEAD_DIM = D_MODEL // N_HEADS
ROT_DIM = HEAD_DIM // 4
ROPE_THETA = 500000.0
MOBA_BLOCK = 256
MOBA_TOPK = 3
POOL_WINDOWS = (2, 4, 8, 16)
POOL_GROUP_DIM = D_MODEL // len(POOL_WINDOWS)
CONV_WIDTH = 3
D_FF = 4 * D_MODEL
NORM_EPS = 1e-6
NEG_INF = -1e30

HALO = 16
ROW_TILE = 512
FF_CHUNK = 1024
ONES_ROWS = 16
VMEM_LIMIT = 56 * 1024 * 1024
Q_SCALE = (1.0 / math.sqrt(HEAD_DIM)) * math.log2(math.e)

F32 = jnp.float32
BF16 = jnp.bfloat16


def _resident(stacked, layer):
    tail = stacked.shape[1:]
    index = (layer,) + (0,) * len(tail)
    return pl.BlockSpec((None,) + tail, lambda *_: index, pipeline_mode=pl.Buffered(1))


def _rms(x, g):
    ms = jnp.mean(x * x, axis=-1, keepdims=True)
    return x * lax.rsqrt(ms + NORM_EPS) * g


def _mlp_residual(x1, g_ref, wup_ref, wdn_ref):
    xn = _rms(x1, g_ref[...]).astype(BF16)
    acc = jnp.zeros_like(x1)
    for c in range(D_FF // FF_CHUNK):
        cs = slice(c * FF_CHUNK, (c + 1) * FF_CHUNK)
        h = jnp.dot(xn, wup_ref[:, cs], preferred_element_type=F32)
        h = jnp.maximum(h, 0.0)
        h = (h * h).astype(BF16)
        acc = acc + jnp.dot(h, wdn_ref[cs, :], preferred_element_type=F32)
    return x1 + acc


def _qkv_kernel(x_ref, g_ref, w_ref, cos_ref, sin_ref, qt_ref, k_ref, vt_ref, km_ref):
    tm = x_ref.shape[0]
    xn = _rms(x_ref[...], g_ref[...]).astype(BF16)
    cos = cos_ref[...]
    sin = sin_ref[...]
    lane = lax.broadcasted_iota(jnp.int32, (tm, HEAD_DIM), 1)
    first_half = lane < (ROT_DIM // 2)

    def rope(t):
        parts = []
        for h in range(N_HEADS):
            th = t[:, h * HEAD_DIM:(h + 1) * HEAD_DIM]
            partner = jnp.where(first_half,
                                pltpu.roll(th, HEAD_DIM - ROT_DIM // 2, 1),
                                pltpu.roll(th, ROT_DIM // 2, 1))
            parts.append(th * cos + partner * sin)
        return jnp.concatenate(parts, axis=1)

    q = jnp.dot(xn, w_ref[:, 0:D_MODEL], preferred_element_type=F32)
    q = rope(q) * Q_SCALE
    for b in range(tm // MOBA_BLOCK):
        qt_ref[b] = q[b * MOBA_BLOCK:(b + 1) * MOBA_BLOCK, :].T.astype(BF16)

    k = jnp.dot(xn, w_ref[:, D_MODEL:2 * D_MODEL], preferred_element_type=F32)
    k = rope(k)
    k_ref[...] = k.astype(BF16)
    for b in range(tm // MOBA_BLOCK):
        km_ref[b] = jnp.mean(k[b * MOBA_BLOCK:(b + 1) * MOBA_BLOCK, :], axis=0, keepdims=True)

    v = jnp.dot(xn, w_ref[:, 2 * D_MODEL:3 * D_MODEL], preferred_element_type=F32)
    for b in range(tm // MOBA_BLOCK):
        vt_ref[b] = v[b * MOBA_BLOCK:(b + 1) * MOBA_BLOCK, :].T.astype(BF16)


def _qkv_call(x, g, w, cos, sin):
    t = x.shape[0]
    tm = ROW_TILE
    nb = tm // MOBA_BLOCK
    return pl.pallas_call(
        _qkv_kernel,
        grid=(t // tm,),
        in_specs=[
            pl.BlockSpec((tm, D_MODEL), lambda i: (i, 0)),
            _resident(*g),
            _resident(*w),
            pl.BlockSpec((tm, HEAD_DIM), lambda i: (i, 0)),
            pl.BlockSpec((tm, HEAD_DIM), lambda i: (i, 0)),
        ],
        out_specs=[
            pl.BlockSpec((nb, D_MODEL, MOBA_BLOCK), lambda i: (i, 0, 0)),
            pl.BlockSpec((tm, D_MODEL), lambda i: (i, 0)),
            pl.BlockSpec((nb, D_MODEL, MOBA_BLOCK), lambda i: (i, 0, 0)),
            pl.BlockSpec((nb, 1, D_MODEL), lambda i: (i, 0, 0)),
        ],
        out_shape=[
            jax.ShapeDtypeStruct((t // MOBA_BLOCK, D_MODEL, MOBA_BLOCK), BF16),
            jax.ShapeDtypeStruct((t, D_MODEL), BF16),
            jax.ShapeDtypeStruct((t // MOBA_BLOCK, D_MODEL, MOBA_BLOCK), BF16),
            jax.ShapeDtypeStruct((t // MOBA_BLOCK, 1, D_MODEL), F32),
        ],
        compiler_params=pltpu.CompilerParams(
            dimension_semantics=("arbitrary",), vmem_limit_bytes=VMEM_LIMIT),
        name="moba_qkv",
    )(x, g[0], w[0], cos, sin)


def _attn_kernel(qt_ref, k_ref, vt_ref, km_ref, o_ref, s_ref, acc_ref, bias_ref, m_ref, l_ref, *,
                 n_blocks):
    i = pl.program_id(1)
    heads = acc_ref.shape[0]
    blk = MOBA_BLOCK

    def hs(h):
        return slice(h * HEAD_DIM, (h + 1) * HEAD_DIM)

    def scores(j, h):
        rows = pl.ds(pl.multiple_of(j * blk, blk), blk)
        return jnp.dot(k_ref[rows, hs(h)], qt_ref[0, hs(h), :], preferred_element_type=F32)

    gates = [jnp.dot(km_ref[:, hs(h)].astype(BF16), qt_ref[0, hs(h), :],
                     preferred_element_type=F32) for h in range(heads)]
    own = []
    for h in range(heads):
        own.append(scores(i, h))
        s_ref[0, h] = scores(0, h)

    row = lax.broadcasted_iota(jnp.int32, (n_blocks, blk), 0)
    valid = row < i
    for h in range(heads):
        gate = jnp.where(valid, gates[h], NEG_INF)
        bias = jnp.full((n_blocks, blk), NEG_INF, F32)
        for _ in range(MOBA_TOPK):
            top = jnp.max(gate, axis=0, keepdims=True)
            first = jnp.min(jnp.where(gate == top, row, n_blocks), axis=0, keepdims=True)
            pick = row == first
            bias = jnp.where(pick, 0.0, bias)
            gate = jnp.where(pick, -jnp.inf, gate)
        bias_ref[h] = jnp.where(valid, bias, NEG_INF)

    kpos = lax.broadcasted_iota(jnp.int32, (blk, blk), 0)
    qpos = lax.broadcasted_iota(jnp.int32, (blk, blk), 1)
    causal = kpos <= qpos
    ones_rows = jnp.ones((ONES_ROWS, blk), BF16)

    def weighted_values(j, h, p):
        vt_aug = jnp.concatenate([vt_ref[j, hs(h), :], ones_rows], axis=0)
        out = jnp.dot(vt_aug, p.astype(BF16), preferred_element_type=F32)
        return out[:HEAD_DIM], out[HEAD_DIM:HEAD_DIM + 1]

    ms, ls = [], []
    for h in range(heads):
        s = jnp.where(causal, own[h], NEG_INF)
        m = jnp.max(s, axis=0, keepdims=True)
        pv, p_sum = weighted_values(i, h, jnp.exp2(s - m))
        acc_ref[h] = pv
        ls.append(p_sum)
        ms.append(m)

    def past_block(h, slot, j, m_old, l_old):
        s = s_ref[slot, h]
        bias = bias_ref[h, pl.ds(j, 1), :]
        m_new = jnp.maximum(m_old, jnp.max(s, axis=0, keepdims=True) + bias)
        alpha = jnp.exp2(m_old - m_new)
        pv, p_sum = weighted_values(j, h, jnp.exp2(s - (m_new - bias)))
        acc_ref[h] = alpha * acc_ref[h] + pv
        return m_new, alpha * l_old + p_sum

    def block_pair(u, carry):
        m_all, l_all = carry
        j0 = 2 * u
        j_next = jnp.minimum(j0 + 2, i - 1)
        ms, ls = [], []
        for h in range(heads):
            m, l = past_block(h, 0, j0, m_all[h:h + 1, :], l_all[h:h + 1, :])
            s_ref[1, h] = scores(j0 + 1, h)
            ms.append(m)
            ls.append(l)
        for h in range(heads):
            ms[h], ls[h] = past_block(h, 1, j0 + 1, ms[h], ls[h])
            s_ref[0, h] = scores(j_next, h)
        return jnp.concatenate(ms, axis=0), jnp.concatenate(ls, axis=0)

    m_all, l_all = lax.fori_loop(
        0, lax.shift_right_logical(i, 1), block_pair,
        (jnp.concatenate(ms, axis=0), jnp.concatenate(ls, axis=0)))
    m_ref[...] = m_all
    l_ref[...] = l_all

    @pl.when((i & 1) == 1)
    def _():
        for h in range(heads):
            m, l = past_block(h, 0, i - 1, m_ref[h:h + 1, :], l_ref[h:h + 1, :])
            m_ref[h:h + 1, :] = m
            l_ref[h:h + 1, :] = l

    for h in range(heads):
        out_t = acc_ref[h] * (1.0 / l_ref[h:h + 1, :])
        o_ref[:, hs(h)] = out_t.T.astype(BF16)


def _attn_call(qt, k, vt, km, batch, seq):
    n_blocks = seq // MOBA_BLOCK
    return pl.pallas_call(
        functools.partial(_attn_kernel, n_blocks=n_blocks),
        grid=(batch, n_blocks),
        in_specs=[
            pl.BlockSpec((1, D_MODEL, MOBA_BLOCK), lambda b, i: (b * n_blocks + i, 0, 0)),
            pl.BlockSpec((seq, D_MODEL), lambda b, i: (b, 0)),
            pl.BlockSpec((n_blocks, D_MODEL, MOBA_BLOCK), lambda b, i: (b, 0, 0)),
            pl.BlockSpec((n_blocks, D_MODEL), lambda b, i: (b, 0)),
        ],
        out_specs=pl.BlockSpec((MOBA_BLOCK, D_MODEL), lambda b, i: (b * n_blocks + i, 0)),
        out_shape=jax.ShapeDtypeStruct((batch * seq, D_MODEL), BF16),
        scratch_shapes=[
            pltpu.VMEM((2, N_HEADS, MOBA_BLOCK, MOBA_BLOCK), F32),
            pltpu.VMEM((N_HEADS, HEAD_DIM, MOBA_BLOCK), F32),
            pltpu.VMEM((N_HEADS, n_blocks, MOBA_BLOCK), F32),
            pltpu.VMEM((N_HEADS, MOBA_BLOCK), F32),
            pltpu.VMEM((N_HEADS, MOBA_BLOCK), F32),
        ],
        compiler_params=pltpu.CompilerParams(
            dimension_semantics=("arbitrary", "arbitrary"), vmem_limit_bytes=VMEM_LIMIT),
        name="moba_attention",
    )(qt, k, vt, km)


def _attn_mlp_kernel(x_ref, a_ref, wo_ref, g_ref, wup_ref, wdn_ref, *rest, final):
    if final:
        gf_ref, o_ref = rest
    else:
        (o_ref,) = rest
    x1 = x_ref[...] + jnp.dot(a_ref[...], wo_ref[...], preferred_element_type=F32)
    out = _mlp_residual(x1, g_ref, wup_ref, wdn_ref)
    if final:
        out = _rms(out, gf_ref[...])
    o_ref[...] = out


def _attn_mlp_call(x, a, wo, g, wup, wdn, g_final=None):
    t = x.shape[0]
    tm = ROW_TILE
    final = g_final is not None
    in_specs = [
        pl.BlockSpec((tm, D_MODEL), lambda i: (i, 0)),
        pl.BlockSpec((tm, D_MODEL), lambda i: (i, 0)),
        _resident(*wo),
        _resident(*g),
        _resident(*wup),
        _resident(*wdn),
    ]
    args = [x, a, wo[0], g[0], wup[0], wdn[0]]
    if final:
        in_specs.append(_resident(*g_final))
        args.append(g_final[0])
    return pl.pallas_call(
        functools.partial(_attn_mlp_kernel, final=final),
        grid=(t // tm,),
        in_specs=in_specs,
        out_specs=pl.BlockSpec((tm, D_MODEL), lambda i: (i, 0)),
        out_shape=jax.ShapeDtypeStruct((t, D_MODEL), F32),
        compiler_params=pltpu.CompilerParams(
            dimension_semantics=("arbitrary",), vmem_limit_bytes=VMEM_LIMIT),
        name="attn_out_mlp",
    )(*args)


def _halo_spec(tm):
    per = tm // HALO
    return pl.BlockSpec((HALO, D_MODEL), lambda i: (jnp.maximum(i * per - 1, 0), 0))


def _pool_mlp_kernel(x_ref, halo_ref, gmix_ref, pw_ref, ps_ref, g_ref, wup_ref, wdn_ref,
                     o_ref, ext_ref, *, tiles_per_seq):
    tm = x_ref.shape[0]
    tile_in_seq = pl.program_id(0) % tiles_per_seq
    x = x_ref[...]
    gmix = gmix_ref[...]
    xn = _rms(x, gmix)
    ext_ref[0:HALO, :] = jnp.where(tile_in_seq == 0, 0.0, _rms(halo_ref[...], gmix))
    ext_ref[HALO:HALO + tm, :] = xn
    pos = tile_in_seq * tm + lax.broadcasted_iota(jnp.int32, (tm, 1), 0)
    ys = []
    for g, w in enumerate(POOL_WINDOWS):
        cs = slice(g * POOL_GROUP_DIM, (g + 1) * POOL_GROUP_DIM)
        own = xn[:, cs]
        total = own
        for back in range(1, w):
            total = total + ext_ref[HALO - back:HALO - back + tm, cs]
        count = jnp.minimum(pos + 1, w).astype(F32)
        pooled = total / count - own
        ys.append(jnp.dot(pooled.astype(BF16), pw_ref[g], preferred_element_type=F32))
    x1 = x + jnp.concatenate(ys, axis=1) * ps_ref[...]
    o_ref[...] = _mlp_residual(x1, g_ref, wup_ref, wdn_ref)


def _pool_mlp_call(x, gmix, pw, ps, g, wup, wdn, seq):
    t = x.shape[0]
    tm = ROW_TILE
    return pl.pallas_call(
        functools.partial(_pool_mlp_kernel, tiles_per_seq=seq // tm),
        grid=(t // tm,),
        in_specs=[
            pl.BlockSpec((tm, D_MODEL), lambda i: (i, 0)),
            _halo_spec(tm),
            _resident(*gmix),
            _resident(*pw),
            _resident(*ps),
            _resident(*g),
            _resident(*wup),
            _resident(*wdn),
        ],
        out_specs=pl.BlockSpec((tm, D_MODEL), lambda i: (i, 0)),
        out_shape=jax.ShapeDtypeStruct((t, D_MODEL), F32),
        scratch_shapes=[pltpu.VMEM((HALO + tm, D_MODEL), F32)],
        compiler_params=pltpu.CompilerParams(
            dimension_semantics=("arbitrary",), vmem_limit_bytes=VMEM_LIMIT),
        name="pool_mlp",
    )(x, x, gmix[0], pw[0], ps[0], g[0], wup[0], wdn[0])


def _conv_mlp_kernel(x_ref, halo_ref, gmix_ref, win_ref, cw_ref, wout_ref, g_ref, wup_ref,
                     wdn_ref, o_ref, xe_ref, z_ref, *, tiles_per_seq):
    tm = x_ref.shape[0]
    tile_in_seq = pl.program_id(0) % tiles_per_seq
    x = x_ref[...]
    gmix = gmix_ref[...]
    xe_ref[0:HALO, :] = _rms(halo_ref[...], gmix).astype(BF16)
    xe_ref[HALO:HALO + tm, :] = _rms(x, gmix).astype(BF16)
    gate_c = jnp.dot(xe_ref[...], win_ref[:, D_MODEL:2 * D_MODEL], preferred_element_type=F32)
    hid = jnp.dot(xe_ref[...], win_ref[:, 2 * D_MODEL:3 * D_MODEL], preferred_element_type=F32)
    z_ref[...] = gate_c * hid

    @pl.when(tile_in_seq == 0)
    def _():
        z_ref[0:HALO, :] = jnp.zeros((HALO, D_MODEL), F32)

    cw = cw_ref[...]
    zc = cw[CONV_WIDTH - 1:CONV_WIDTH, :] * z_ref[HALO:HALO + tm, :]
    for back in range(1, CONV_WIDTH):
        tap = cw[CONV_WIDTH - 1 - back:CONV_WIDTH - back, :]
        zc = zc + tap * z_ref[HALO - back:HALO - back + tm, :]
    gate_b = jnp.dot(xe_ref[HALO:HALO + tm, :], win_ref[:, 0:D_MODEL],
                     preferred_element_type=F32)
    y = jnp.dot((gate_b * zc).astype(BF16), wout_ref[...], preferred_element_type=F32)
    o_ref[...] = _mlp_residual(x + y, g_ref, wup_ref, wdn_ref)


def _conv_mlp_call(x, gmix, win, cw, wout, g, wup, wdn, seq):
    t = x.shape[0]
    tm = ROW_TILE
    return pl.pallas_call(
        functools.partial(_conv_mlp_kernel, tiles_per_seq=seq // tm),
        grid=(t // tm,),
        in_specs=[
            pl.BlockSpec((tm, D_MODEL), lambda i: (i, 0)),
            _halo_spec(tm),
            _resident(*gmix),
            _resident(*win),
            _resident(*cw),
            _resident(*wout),
            _resident(*g),
            _resident(*wup),
            _resident(*wdn),
        ],
        out_specs=pl.BlockSpec((tm, D_MODEL), lambda i: (i, 0)),
        out_shape=jax.ShapeDtypeStruct((t, D_MODEL), F32),
        scratch_shapes=[
            pltpu.VMEM((HALO + tm, D_MODEL), BF16),
            pltpu.VMEM((HALO + tm, D_MODEL), F32),
        ],
        compiler_params=pltpu.CompilerParams(
            dimension_semantics=("arbitrary",), vmem_limit_bytes=VMEM_LIMIT),
        name="conv_mlp",
    )(x, x, gmix[0], win[0], cw[0], wout[0], g[0], wup[0], wdn[0])


def _rope_tables(positions):
    inv_freq = ROPE_THETA ** (-jnp.arange(0, ROT_DIM, 2, dtype=F32) / ROT_DIM)
    ang = positions.astype(F32).reshape(-1, 1) * inv_freq
    c, s = jnp.cos(ang), jnp.sin(ang)
    rest = HEAD_DIM - ROT_DIM
    n = ang.shape[0]
    cos = jnp.concatenate([c, c, jnp.ones((n, rest), F32)], axis=1)
    sin = jnp.concatenate([-s, s, jnp.zeros((n, rest), F32)], axis=1)
    return cos, sin


def kernel(x, positions, norm_mix, norm_mlp, attn_w_qkv, attn_w_o, pool_w, pool_scale,
           conv_w_in, conv_w, conv_w_out, mlp_w_up, mlp_w_down, norm_final):
    batch, seq, d = x.shape
    depth = norm_mix.shape[0]
    assert d == D_MODEL and seq % ROW_TILE == 0 and ROW_TILE % MOBA_BLOCK == 0
    assert (depth - 1) % 3 == 0, "the final norm is fused into a MoBA layer's MLP stage"
    cos, sin = _rope_tables(positions)
    h = x.reshape(batch * seq, d)
    rows = lambda v: v.reshape(v.shape[0], 1, v.shape[-1])
    g_mix, g_mlp, p_scale = rows(norm_mix), rows(norm_mlp), rows(pool_scale)
    g_final = norm_final.reshape(1, 1, d)
    w_qkv, w_o = attn_w_qkv.astype(BF16), attn_w_o.astype(BF16)
    w_pool = pool_w.astype(BF16)
    w_cin, w_cout = conv_w_in.astype(BF16), conv_w_out.astype(BF16)
    w_up, w_dn = mlp_w_up.astype(BF16), mlp_w_down.astype(BF16)
    i_attn = i_pool = i_conv = 0
    for i in range(depth):
        kind = i % 3
        mlp = ((g_mlp, i), (w_up, i), (w_dn, i))
        if kind == 0:
            qt, k, vt, km = _qkv_call(h, (g_mix, i), (w_qkv, i_attn), cos, sin)
            a = _attn_call(qt, k, vt, km.reshape(-1, d), batch, seq)
            h = _attn_mlp_call(h, a, (w_o, i_attn), *mlp,
                               g_final=(g_final, 0) if i == depth - 1 else None)
            i_attn += 1
        elif kind == 1:
            h = _pool_mlp_call(h, (g_mix, i), (w_pool, i_pool), (p_scale, i_pool), *mlp, seq)
            i_pool += 1
        else:
            h = _conv_mlp_call(h, (g_mix, i), (w_cin, i_conv), (conv_w, i_conv),
                               (w_cout, i_conv), *mlp, seq)
            i_conv += 1
    return h.reshape(batch, seq, d)
```

```python
import functools
import math

import jax
import jax.numpy as jnp
from jax import lax
from jax.experimental import pallas as pl
from jax.experimental.pallas import tpu as pltpu

D_MODEL = 1024
N_HEADS = 8
HEAD_DIM = D_MODEL // N_HEADS
ROT_DIM = HEAD_DIM // 4
ROPE_THETA = 500000.0
MOBA_BLOCK = 256
MOBA_TOPK = 3
POOL_WINDOWS = (2, 4, 8, 16)
POOL_GROUP_DIM = D_MODEL // len(POOL_WINDOWS)
CONV_WIDTH = 3
D_FF = 4 * D_MODEL
NORM_EPS = 1e-6
NEG_INF = -1e30

HALO = 16
ROW_TILE = 512
FF_CHUNK = 1024
ONES_ROWS = 16
STAGE_BYTES = 512 * 1024
ANY_SPEC = pl.BlockSpec(memory_space=pl.ANY)
VMEM_LIMIT = 56 * 1024 * 1024
Q_SCALE = (1.0 / math.sqrt(HEAD_DIM)) * math.log2(math.e)

F32 = jnp.float32
BF16 = jnp.bfloat16


def _resident(stacked, layer):
    tail = stacked.shape[1:]
    index = (layer,) + (0,) * len(tail)
    return pl.BlockSpec((None,) + tail, lambda *_: index, pipeline_mode=pl.Buffered(1))


def _chunk_rows(width):
    rows = STAGE_BYTES // (4 * width)
    return rows - rows % 16


def _weight_scratch(w):
    rows, width = w[0].shape[1:]
    return [pltpu.VMEM((rows, width), BF16),
            pltpu.VMEM((2, _chunk_rows(width), width), F32),
            pltpu.SemaphoreType.DMA((2,))]


def _fetch_weight(w_hbm, layer, w_vmem, stage, sem):
    rows = w_vmem.shape[0]
    r = stage.shape[1]
    assert rows % r == 0
    n = rows // r

    def chunk(c, slot):
        return pltpu.make_async_copy(w_hbm.at[layer, pl.ds(c * r, r), :], stage.at[slot],
                                     sem.at[slot])

    chunk(0, 0).start()

    def step(c, carry):
        slot = c % 2

        @pl.when(c + 1 < n)
        def _():
            chunk(c + 1, 1 - slot).start()

        chunk(c, slot).wait()
        w_vmem[pl.ds(pl.multiple_of(c * r, r), r), :] = stage[slot].astype(BF16)
        return carry

    lax.fori_loop(0, n, step, 0)


def _with_weights(body, n_in, layers, n_out):
    n_w = len(layers)

    def kernel(*refs):
        ins = refs[:n_in]
        hbm = refs[n_in:n_in + n_w]
        outs = refs[n_in + n_w:n_in + n_w + n_out]
        scratch = refs[n_in + n_w + n_out:]
        fetched = [scratch[3 * k:3 * k + 3] for k in range(n_w)]

        @pl.when(pl.program_id(0) == 0)
        def _():
            for w_hbm, layer, (w_vmem, stage, sem) in zip(hbm, layers, fetched):
                _fetch_weight(w_hbm, layer, w_vmem, stage, sem)

        body(*ins, *[f[0] for f in fetched], *outs, *scratch[3 * n_w:])

    return kernel


def _rms(x, g):
    ms = jnp.mean(x * x, axis=-1, keepdims=True)
    return x * lax.rsqrt(ms + NORM_EPS) * g


def _mlp_residual(x1, g_ref, wup_ref, wdn_ref):
    xn = _rms(x1, g_ref[...]).astype(BF16)
    acc = jnp.zeros_like(x1)
    for c in range(D_FF // FF_CHUNK):
        cs = slice(c * FF_CHUNK, (c + 1) * FF_CHUNK)
        h = jnp.dot(xn, wup_ref[:, cs], preferred_element_type=F32)
        h = jnp.maximum(h, 0.0)
        h = (h * h).astype(BF16)
        acc = acc + jnp.dot(h, wdn_ref[cs, :], preferred_element_type=F32)
    return x1 + acc


def _qkv_kernel(x_ref, g_ref, cos_ref, sin_ref, w_ref, qt_ref, k_ref, vt_ref, km_ref):
    tm = x_ref.shape[0]
    xn = _rms(x_ref[...], g_ref[...]).astype(BF16)
    cos = cos_ref[...]
    sin = sin_ref[...]
    lane = lax.broadcasted_iota(jnp.int32, (tm, HEAD_DIM), 1)
    first_half = lane < (ROT_DIM // 2)

    def rope(t):
        parts = []
        for h in range(N_HEADS):
            th = t[:, h * HEAD_DIM:(h + 1) * HEAD_DIM]
            partner = jnp.where(first_half,
                                pltpu.roll(th, HEAD_DIM - ROT_DIM // 2, 1),
                                pltpu.roll(th, ROT_DIM // 2, 1))
            parts.append(th * cos + partner * sin)
        return jnp.concatenate(parts, axis=1)

    q = jnp.dot(xn, w_ref[:, 0:D_MODEL], preferred_element_type=F32)
    q = rope(q) * Q_SCALE
    for b in range(tm // MOBA_BLOCK):
        qt_ref[b] = q[b * MOBA_BLOCK:(b + 1) * MOBA_BLOCK, :].T.astype(BF16)

    k = jnp.dot(xn, w_ref[:, D_MODEL:2 * D_MODEL], preferred_element_type=F32)
    k = rope(k)
    k_ref[...] = k.astype(BF16)
    for b in range(tm // MOBA_BLOCK):
        km_ref[b] = jnp.mean(k[b * MOBA_BLOCK:(b + 1) * MOBA_BLOCK, :], axis=0, keepdims=True)

    v = jnp.dot(xn, w_ref[:, 2 * D_MODEL:3 * D_MODEL], preferred_element_type=F32)
    for b in range(tm // MOBA_BLOCK):
        vt_ref[b] = v[b * MOBA_BLOCK:(b + 1) * MOBA_BLOCK, :].T.astype(BF16)


def _qkv_call(x, g, w, cos, sin):
    t = x.shape[0]
    tm = ROW_TILE
    nb = tm // MOBA_BLOCK
    return pl.pallas_call(
        _with_weights(_qkv_kernel, n_in=4, layers=(w[1],), n_out=4),
        grid=(t // tm,),
        in_specs=[
            pl.BlockSpec((tm, D_MODEL), lambda i: (i, 0)),
            _resident(*g),
            pl.BlockSpec((tm, HEAD_DIM), lambda i: (i, 0)),
            pl.BlockSpec((tm, HEAD_DIM), lambda i: (i, 0)),
            ANY_SPEC,
        ],
        out_specs=[
            pl.BlockSpec((nb, D_MODEL, MOBA_BLOCK), lambda i: (i, 0, 0)),
            pl.BlockSpec((tm, D_MODEL), lambda i: (i, 0)),
            pl.BlockSpec((nb, D_MODEL, MOBA_BLOCK), lambda i: (i, 0, 0)),
            pl.BlockSpec((nb, 1, D_MODEL), lambda i: (i, 0, 0)),
        ],
        out_shape=[
            jax.ShapeDtypeStruct((t // MOBA_BLOCK, D_MODEL, MOBA_BLOCK), BF16),
            jax.ShapeDtypeStruct((t, D_MODEL), BF16),
            jax.ShapeDtypeStruct((t // MOBA_BLOCK, D_MODEL, MOBA_BLOCK), BF16),
            jax.ShapeDtypeStruct((t // MOBA_BLOCK, 1, D_MODEL), F32),
        ],
        scratch_shapes=_weight_scratch(w),
        compiler_params=pltpu.CompilerParams(
            dimension_semantics=("arbitrary",), vmem_limit_bytes=VMEM_LIMIT),
        name="moba_qkv",
    )(x, g[0], cos, sin, w[0])


def _attn_kernel(qt_ref, k_ref, vt_ref, km_ref, o_ref, s_ref, acc_ref, bias_ref, m_ref, l_ref, *,
                 n_blocks):
    i = pl.program_id(1)
    heads = acc_ref.shape[0]
    blk = MOBA_BLOCK

    def hs(h):
        return slice(h * HEAD_DIM, (h + 1) * HEAD_DIM)

    def scores(j, h):
        rows = pl.ds(pl.multiple_of(j * blk, blk), blk)
        return jnp.dot(k_ref[rows, hs(h)], qt_ref[0, hs(h), :], preferred_element_type=F32)

    gates = [jnp.dot(km_ref[:, hs(h)].astype(BF16), qt_ref[0, hs(h), :],
                     preferred_element_type=F32) for h in range(heads)]
    own = []
    for h in range(heads):
        own.append(scores(i, h))
        s_ref[0, h] = scores(0, h)

    row = lax.broadcasted_iota(jnp.int32, (n_blocks, blk), 0)
    valid = row < i
    for h in range(heads):
        gate = jnp.where(valid, gates[h], NEG_INF)
        bias = jnp.full((n_blocks, blk), NEG_INF, F32)
        for _ in range(MOBA_TOPK):
            top = jnp.max(gate, axis=0, keepdims=True)
            first = jnp.min(jnp.where(gate == top, row, n_blocks), axis=0, keepdims=True)
            pick = row == first
            bias = jnp.where(pick, 0.0, bias)
            gate = jnp.where(pick, -jnp.inf, gate)
        bias_ref[h] = jnp.where(valid, bias, NEG_INF)

    kpos = lax.broadcasted_iota(jnp.int32, (blk, blk), 0)
    qpos = lax.broadcasted_iota(jnp.int32, (blk, blk), 1)
    causal = kpos <= qpos
    ones_rows = jnp.ones((ONES_ROWS, blk), BF16)

    def weighted_values(j, h, p):
        vt_aug = jnp.concatenate([vt_ref[j, hs(h), :], ones_rows], axis=0)
        out = jnp.dot(vt_aug, p.astype(BF16), preferred_element_type=F32)
        return out[:HEAD_DIM], out[HEAD_DIM:HEAD_DIM + 1]

    ms, ls = [], []
    for h in range(heads):
        s = jnp.where(causal, own[h], NEG_INF)
        m = jnp.max(s, axis=0, keepdims=True)
        pv, p_sum = weighted_values(i, h, jnp.exp2(s - m))
        acc_ref[h] = pv
        ls.append(p_sum)
        ms.append(m)

    def past_block(h, slot, j, m_old, l_old):
        s = s_ref[slot, h]
        bias = bias_ref[h, pl.ds(j, 1), :]
        m_new = jnp.maximum(m_old, jnp.max(s, axis=0, keepdims=True) + bias)
        alpha = jnp.exp2(m_old - m_new)
        pv, p_sum = weighted_values(j, h, jnp.exp2(s - (m_new - bias)))
        acc_ref[h] = alpha * acc_ref[h] + pv
        return m_new, alpha * l_old + p_sum

    def block_pair(u, carry):
        m_all, l_all = carry
        j0 = 2 * u
        j_next = jnp.minimum(j0 + 2, i - 1)
        ms, ls = [], []
        for h in range(heads):
            m, l = past_block(h, 0, j0, m_all[h:h + 1, :], l_all[h:h + 1, :])
            s_ref[1, h] = scores(j0 + 1, h)
            ms.append(m)
            ls.append(l)
        for h in range(heads):
            ms[h], ls[h] = past_block(h, 1, j0 + 1, ms[h], ls[h])
            s_ref[0, h] = scores(j_next, h)
        return jnp.concatenate(ms, axis=0), jnp.concatenate(ls, axis=0)

    m_all, l_all = lax.fori_loop(
        0, lax.shift_right_logical(i, 1), block_pair,
        (jnp.concatenate(ms, axis=0), jnp.concatenate(ls, axis=0)))
    m_ref[...] = m_all
    l_ref[...] = l_all

    @pl.when((i & 1) == 1)
    def _():
        for h in range(heads):
            m, l = past_block(h, 0, i - 1, m_ref[h:h + 1, :], l_ref[h:h + 1, :])
            m_ref[h:h + 1, :] = m
            l_ref[h:h + 1, :] = l

    for h in range(heads):
        out_t = acc_ref[h] * (1.0 / l_ref[h:h + 1, :])
        o_ref[:, hs(h)] = out_t.T.astype(BF16)


def _attn_call(qt, k, vt, km, batch, seq):
    n_blocks = seq // MOBA_BLOCK
    return pl.pallas_call(
        functools.partial(_attn_kernel, n_blocks=n_blocks),
        grid=(batch, n_blocks),
        in_specs=[
            pl.BlockSpec((1, D_MODEL, MOBA_BLOCK), lambda b, i: (b * n_blocks + i, 0, 0)),
            pl.BlockSpec((seq, D_MODEL), lambda b, i: (b, 0)),
            pl.BlockSpec((n_blocks, D_MODEL, MOBA_BLOCK), lambda b, i: (b, 0, 0)),
            pl.BlockSpec((n_blocks, D_MODEL), lambda b, i: (b, 0)),
        ],
        out_specs=pl.BlockSpec((MOBA_BLOCK, D_MODEL), lambda b, i: (b * n_blocks + i, 0)),
        out_shape=jax.ShapeDtypeStruct((batch * seq, D_MODEL), BF16),
        scratch_shapes=[
            pltpu.VMEM((2, N_HEADS, MOBA_BLOCK, MOBA_BLOCK), F32),
            pltpu.VMEM((N_HEADS, HEAD_DIM, MOBA_BLOCK), F32),
            pltpu.VMEM((N_HEADS, n_blocks, MOBA_BLOCK), F32),
            pltpu.VMEM((N_HEADS, MOBA_BLOCK), F32),
            pltpu.VMEM((N_HEADS, MOBA_BLOCK), F32),
        ],
        compiler_params=pltpu.CompilerParams(
            dimension_semantics=("arbitrary", "arbitrary"), vmem_limit_bytes=VMEM_LIMIT),
        name="moba_attention",
    )(qt, k, vt, km)


def _attn_mlp_kernel(x_ref, a_ref, g_ref, *rest, final):
    if final:
        gf_ref, wo_ref, wup_ref, wdn_ref, o_ref = rest
    else:
        wo_ref, wup_ref, wdn_ref, o_ref = rest
    x1 = x_ref[...] + jnp.dot(a_ref[...], wo_ref[...], preferred_element_type=F32)
    out = _mlp_residual(x1, g_ref, wup_ref, wdn_ref)
    if final:
        out = _rms(out, gf_ref[...])
    o_ref[...] = out


def _attn_mlp_call(x, a, wo, g, wup, wdn, g_final=None):
    t = x.shape[0]
    tm = ROW_TILE
    final = g_final is not None
    in_specs = [
        pl.BlockSpec((tm, D_MODEL), lambda i: (i, 0)),
        pl.BlockSpec((tm, D_MODEL), lambda i: (i, 0)),
        _resident(*g),
    ]
    args = [x, a, g[0]]
    if final:
        in_specs.append(_resident(*g_final))
        args.append(g_final[0])
    weights = (wo, wup, wdn)
    return pl.pallas_call(
        _with_weights(functools.partial(_attn_mlp_kernel, final=final), n_in=len(args),
                      layers=[w[1] for w in weights], n_out=1),
        grid=(t // tm,),
        in_specs=in_specs + [ANY_SPEC] * len(weights),
        out_specs=pl.BlockSpec((tm, D_MODEL), lambda i: (i, 0)),
        out_shape=jax.ShapeDtypeStruct((t, D_MODEL), F32),
        scratch_shapes=[s for w in weights for s in _weight_scratch(w)],
        compiler_params=pltpu.CompilerParams(
            dimension_semantics=("arbitrary",), vmem_limit_bytes=VMEM_LIMIT),
        name="attn_out_mlp",
    )(*args, *[w[0] for w in weights])


def _halo_spec(tm):
    per = tm // HALO
    return pl.BlockSpec((HALO, D_MODEL), lambda i: (jnp.maximum(i * per - 1, 0), 0))


def _pool_mlp_kernel(x_ref, halo_ref, gmix_ref, pw_ref, ps_ref, g_ref, wup_ref, wdn_ref,
                     o_ref, ext_ref, *, tiles_per_seq):
    tm = x_ref.shape[0]
    tile_in_seq = pl.program_id(0) % tiles_per_seq
    x = x_ref[...]
    gmix = gmix_ref[...]
    xn = _rms(x, gmix)
    ext_ref[0:HALO, :] = jnp.where(tile_in_seq == 0, 0.0, _rms(halo_ref[...], gmix))
    ext_ref[HALO:HALO + tm, :] = xn
    pos = tile_in_seq * tm + lax.broadcasted_iota(jnp.int32, (tm, 1), 0)
    ys = []
    for g, w in enumerate(POOL_WINDOWS):
        cs = slice(g * POOL_GROUP_DIM, (g + 1) * POOL_GROUP_DIM)
        own = xn[:, cs]
        total = own
        for back in range(1, w):
            total = total + ext_ref[HALO - back:HALO - back + tm, cs]
        count = jnp.minimum(pos + 1, w).astype(F32)
        pooled = total / count - own
        ys.append(jnp.dot(pooled.astype(BF16), pw_ref[g], preferred_element_type=F32))
    x1 = x + jnp.concatenate(ys, axis=1) * ps_ref[...]
    o_ref[...] = _mlp_residual(x1, g_ref, wup_ref, wdn_ref)


def _pool_mlp_call(x, gmix, pw, ps, g, wup, wdn, seq):
    t = x.shape[0]
    tm = ROW_TILE
    weights = (wup, wdn)
    return pl.pallas_call(
        _with_weights(functools.partial(_pool_mlp_kernel, tiles_per_seq=seq // tm), n_in=6,
                      layers=[w[1] for w in weights], n_out=1),
        grid=(t // tm,),
        in_specs=[
            pl.BlockSpec((tm, D_MODEL), lambda i: (i, 0)),
            _halo_spec(tm),
            _resident(*gmix),
            _resident(*pw),
            _resident(*ps),
            _resident(*g),
        ] + [ANY_SPEC] * len(weights),
        out_specs=pl.BlockSpec((tm, D_MODEL), lambda i: (i, 0)),
        out_shape=jax.ShapeDtypeStruct((t, D_MODEL), F32),
        scratch_shapes=[s for w in weights for s in _weight_scratch(w)]
        + [pltpu.VMEM((HALO + tm, D_MODEL), F32)],
        compiler_params=pltpu.CompilerParams(
            dimension_semantics=("arbitrary",), vmem_limit_bytes=VMEM_LIMIT),
        name="pool_mlp",
    )(x, x, gmix[0], pw[0], ps[0], g[0], *[w[0] for w in weights])


def _conv_mlp_kernel(x_ref, halo_ref, gmix_ref, cw_ref, g_ref, win_ref, wout_ref, wup_ref,
                     wdn_ref, o_ref, xe_ref, z_ref, *, tiles_per_seq):
    tm = x_ref.shape[0]
    tile_in_seq = pl.program_id(0) % tiles_per_seq
    x = x_ref[...]
    gmix = gmix_ref[...]
    xe_ref[0:HALO, :] = _rms(halo_ref[...], gmix).astype(BF16)
    xe_ref[HALO:HALO + tm, :] = _rms(x, gmix).astype(BF16)
    gate_c = jnp.dot(xe_ref[...], win_ref[:, D_MODEL:2 * D_MODEL], preferred_element_type=F32)
    hid = jnp.dot(xe_ref[...], win_ref[:, 2 * D_MODEL:3 * D_MODEL], preferred_element_type=F32)
    z_ref[...] = gate_c * hid

    @pl.when(tile_in_seq == 0)
    def _():
        z_ref[0:HALO, :] = jnp.zeros((HALO, D_MODEL), F32)

    cw = cw_ref[...]
    zc = cw[CONV_WIDTH - 1:CONV_WIDTH, :] * z_ref[HALO:HALO + tm, :]
    for back in range(1, CONV_WIDTH):
        tap = cw[CONV_WIDTH - 1 - back:CONV_WIDTH - back, :]
        zc = zc + tap * z_ref[HALO - back:HALO - back + tm, :]
    gate_b = jnp.dot(xe_ref[HALO:HALO + tm, :], win_ref[:, 0:D_MODEL],
                     preferred_element_type=F32)
    y = jnp.dot((gate_b * zc).astype(BF16), wout_ref[...], preferred_element_type=F32)
    o_ref[...] = _mlp_residual(x + y, g_ref, wup_ref, wdn_ref)


def _conv_mlp_call(x, gmix, win, cw, wout, g, wup, wdn, seq):
    t = x.shape[0]
    tm = ROW_TILE
    weights = (win, wout, wup, wdn)
    return pl.pallas_call(
        _with_weights(functools.partial(_conv_mlp_kernel, tiles_per_seq=seq // tm), n_in=5,
                      layers=[w[1] for w in weights], n_out=1),
        grid=(t // tm,),
        in_specs=[
            pl.BlockSpec((tm, D_MODEL), lambda i: (i, 0)),
            _halo_spec(tm),
            _resident(*gmix),
            _resident(*cw),
            _resident(*g),
        ] + [ANY_SPEC] * len(weights),
        out_specs=pl.BlockSpec((tm, D_MODEL), lambda i: (i, 0)),
        out_shape=jax.ShapeDtypeStruct((t, D_MODEL), F32),
        scratch_shapes=[s for w in weights for s in _weight_scratch(w)] + [
            pltpu.VMEM((HALO + tm, D_MODEL), BF16),
            pltpu.VMEM((HALO + tm, D_MODEL), F32),
        ],
        compiler_params=pltpu.CompilerParams(
            dimension_semantics=("arbitrary",), vmem_limit_bytes=VMEM_LIMIT),
        name="conv_mlp",
    )(x, x, gmix[0], cw[0], g[0], *[w[0] for w in weights])


def _rope_angle_kernel(pos_ref, freq_ref, cos_ref, sin_ref):
    pos = pos_ref[...].astype(F32)
    for f in range(ROT_DIM // 2):
        ang = pos * freq_ref[f]
        cos_ref[f] = jnp.cos(ang)
        sin_ref[f] = jnp.sin(ang)


def _rope_tables(positions):
    n = positions.size
    lanes = HEAD_DIM
    n_freq = ROT_DIM // 2
    inv_freq = ROPE_THETA ** (-jnp.arange(0, ROT_DIM, 2, dtype=F32) / ROT_DIM)
    slab = jax.ShapeDtypeStruct((n_freq, n // lanes, lanes), F32)
    cos_t, sin_t = pl.pallas_call(
        _rope_angle_kernel,
        in_specs=[pl.BlockSpec(memory_space=pltpu.VMEM), pl.BlockSpec(memory_space=pltpu.SMEM)],
        out_specs=[pl.BlockSpec(memory_space=pltpu.VMEM)] * 2,
        out_shape=[slab, slab],
        name="rope_angles",
    )(positions.reshape(n // lanes, lanes), inv_freq)
    c = cos_t.reshape(n_freq, n).T
    s = sin_t.reshape(n_freq, n).T
    rest = HEAD_DIM - ROT_DIM
    cos = jnp.concatenate([c, c, jnp.ones((n, rest), F32)], axis=1)
    sin = jnp.concatenate([-s, s, jnp.zeros((n, rest), F32)], axis=1)
    return cos, sin


def kernel(x, positions, norm_mix, norm_mlp, attn_w_qkv, attn_w_o, pool_w, pool_scale,
           conv_w_in, conv_w, conv_w_out, mlp_w_up, mlp_w_down, norm_final):
    batch, seq, d = x.shape
    depth = norm_mix.shape[0]
    assert d == D_MODEL and seq % ROW_TILE == 0 and ROW_TILE % MOBA_BLOCK == 0
    assert (depth - 1) % 3 == 0, "the final norm is fused into a MoBA layer's MLP stage"
    cos, sin = _rope_tables(positions)
    h = x.reshape(batch * seq, d)
    rows = lambda v: v.reshape(v.shape[0], 1, v.shape[-1])
    g_mix, g_mlp, p_scale = rows(norm_mix), rows(norm_mlp), rows(pool_scale)
    g_final = norm_final.reshape(1, 1, d)
    w_pool = pool_w.astype(BF16)
    i_attn = i_pool = i_conv = 0
    for i in range(depth):
        kind = i % 3
        mlp = ((g_mlp, i), (mlp_w_up, i), (mlp_w_down, i))
        if kind == 0:
            qt, k, vt, km = _qkv_call(h, (g_mix, i), (attn_w_qkv, i_attn), cos, sin)
            a = _attn_call(qt, k, vt, km.reshape(-1, d), batch, seq)
            h = _attn_mlp_call(h, a, (attn_w_o, i_attn), *mlp,
                               g_final=(g_final, 0) if i == depth - 1 else None)
            i_attn += 1
        elif kind == 1:
            h = _pool_mlp_call(h, (g_mix, i), (w_pool, i_pool), (p_scale, i_pool), *mlp, seq)
            i_pool += 1
        else:
            h = _conv_mlp_call(h, (g_mix, i), (conv_w_in, i_conv), (conv_w, i_conv),
                               (conv_w_out, i_conv), *mlp, seq)
            i_conv += 1
    return h.reshape(batch, seq, d)
```

```python
import functools
import math

import jax
import jax.numpy as jnp
from jax import lax
from jax.experimental import pallas as pl
from jax.experimental.pallas import tpu as pltpu

D_MODEL = 1024
N_HEADS = 8
HEAD_DIM = D_MODEL // N_HEADS
ROT_DIM = HEAD_DIM // 4
ROPE_THETA = 500000.0
MOBA_BLOCK = 256
MOBA_TOPK = 3
POOL_WINDOWS = (2, 4, 8, 16)
POOL_GROUP_DIM = D_MODEL // len(POOL_WINDOWS)
CONV_WIDTH = 3
D_FF = 4 * D_MODEL
NORM_EPS = 1e-6
NEG_INF = -1e30

HALO = 16
ROW_TILE = 512
FF_CHUNK = 1024
ONES_ROWS = 16
STAGE_BYTES = 512 * 1024
STAGE_SLOTS = 4
ANY_SPEC = pl.BlockSpec(memory_space=pl.ANY)
VMEM_LIMIT = 56 * 1024 * 1024
Q_SCALE = (1.0 / math.sqrt(HEAD_DIM)) * math.log2(math.e)

F32 = jnp.float32
BF16 = jnp.bfloat16


def _resident(stacked, layer):
    tail = stacked.shape[1:]
    index = (layer,) + (0,) * len(tail)
    return pl.BlockSpec((None,) + tail, lambda *_: index, pipeline_mode=pl.Buffered(1))


def _chunk_rows(width):
    rows = STAGE_BYTES // (4 * width)
    return rows - rows % 16


def _weight_scratch(w):
    rows, width = w[0].shape[1:]
    return [pltpu.VMEM((rows, width), BF16),
            pltpu.VMEM((STAGE_SLOTS, _chunk_rows(width), width), F32),
            pltpu.SemaphoreType.DMA((STAGE_SLOTS,))]


def _fetch_weight(w_hbm, layer, w_vmem, stage, sem):
    rows = w_vmem.shape[0]
    slots, r = stage.shape[:2]
    assert rows % r == 0 and rows // r >= slots
    n = rows // r

    def chunk(c):
        slot = c % slots
        return pltpu.make_async_copy(w_hbm.at[layer, pl.ds(c * r, r), :], stage.at[slot],
                                     sem.at[slot])

    for c in range(slots - 1):
        chunk(c).start()

    def step(c, carry):
        @pl.when(c + slots - 1 < n)
        def _():
            chunk(c + slots - 1).start()

        chunk(c).wait()
        w_vmem[pl.ds(pl.multiple_of(c * r, r), r), :] = stage[c % slots].astype(BF16)
        return carry

    lax.fori_loop(0, n, step, 0)


def _with_weights(body, n_in, layers, n_out):
    n_w = len(layers)

    def kernel(*refs):
        ins = refs[:n_in]
        hbm = refs[n_in:n_in + n_w]
        outs = refs[n_in + n_w:n_in + n_w + n_out]
        scratch = refs[n_in + n_w + n_out:]
        fetched = [scratch[3 * k:3 * k + 3] for k in range(n_w)]

        @pl.when(pl.program_id(0) == 0)
        def _():
            for w_hbm, layer, (w_vmem, stage, sem) in zip(hbm, layers, fetched):
                _fetch_weight(w_hbm, layer, w_vmem, stage, sem)

        body(*ins, *[f[0] for f in fetched], *outs, *scratch[3 * n_w:])

    return kernel


def _rms(x, g):
    ms = jnp.mean(x * x, axis=-1, keepdims=True)
    return x * lax.rsqrt(ms + NORM_EPS) * g


def _mlp_residual(x1, g_ref, wup_ref, wdn_ref):
    xn = _rms(x1, g_ref[...]).astype(BF16)
    acc = jnp.zeros_like(x1)
    for c in range(D_FF // FF_CHUNK):
        cs = slice(c * FF_CHUNK, (c + 1) * FF_CHUNK)
        h = jnp.dot(xn, wup_ref[:, cs], preferred_element_type=F32)
        h = jnp.maximum(h, 0.0)
        h = (h * h).astype(BF16)
        acc = acc + jnp.dot(h, wdn_ref[cs, :], preferred_element_type=F32)
    return x1 + acc


def _qkv_kernel(x_ref, g_ref, cos_ref, sin_ref, w_ref, qt_ref, k_ref, vt_ref, km_ref):
    tm = x_ref.shape[0]
    xn = _rms(x_ref[...], g_ref[...]).astype(BF16)
    cos = cos_ref[...]
    sin = sin_ref[...]
    lane = lax.broadcasted_iota(jnp.int32, (tm, HEAD_DIM), 1)
    first_half = lane < (ROT_DIM // 2)

    def rope(t):
        parts = []
        for h in range(N_HEADS):
            th = t[:, h * HEAD_DIM:(h + 1) * HEAD_DIM]
            partner = jnp.where(first_half,
                                pltpu.roll(th, HEAD_DIM - ROT_DIM // 2, 1),
                                pltpu.roll(th, ROT_DIM // 2, 1))
            parts.append(th * cos + partner * sin)
        return jnp.concatenate(parts, axis=1)

    q = jnp.dot(xn, w_ref[:, 0:D_MODEL], preferred_element_type=F32)
    q = rope(q) * Q_SCALE
    for b in range(tm // MOBA_BLOCK):
        qt_ref[b] = q[b * MOBA_BLOCK:(b + 1) * MOBA_BLOCK, :].T.astype(BF16)

    k = jnp.dot(xn, w_ref[:, D_MODEL:2 * D_MODEL], preferred_element_type=F32)
    k = rope(k)
    k_ref[...] = k.astype(BF16)
    for b in range(tm // MOBA_BLOCK):
        km_ref[b] = jnp.mean(k[b * MOBA_BLOCK:(b + 1) * MOBA_BLOCK, :], axis=0, keepdims=True)

    v = jnp.dot(xn, w_ref[:, 2 * D_MODEL:3 * D_MODEL], preferred_element_type=F32)
    for b in range(tm // MOBA_BLOCK):
        vt_ref[b] = v[b * MOBA_BLOCK:(b + 1) * MOBA_BLOCK, :].T.astype(BF16)


def _qkv_call(x, g, w, cos, sin):
    t = x.shape[0]
    tm = ROW_TILE
    nb = tm // MOBA_BLOCK
    return pl.pallas_call(
        _with_weights(_qkv_kernel, n_in=4, layers=(w[1],), n_out=4),
        grid=(t // tm,),
        in_specs=[
            pl.BlockSpec((tm, D_MODEL), lambda i: (i, 0)),
            _resident(*g),
            pl.BlockSpec((tm, HEAD_DIM), lambda i: (i, 0)),
            pl.BlockSpec((tm, HEAD_DIM), lambda i: (i, 0)),
            ANY_SPEC,
        ],
        out_specs=[
            pl.BlockSpec((nb, D_MODEL, MOBA_BLOCK), lambda i: (i, 0, 0)),
            pl.BlockSpec((tm, D_MODEL), lambda i: (i, 0)),
            pl.BlockSpec((nb, D_MODEL, MOBA_BLOCK), lambda i: (i, 0, 0)),
            pl.BlockSpec((nb, 1, D_MODEL), lambda i: (i, 0, 0)),
        ],
        out_shape=[
            jax.ShapeDtypeStruct((t // MOBA_BLOCK, D_MODEL, MOBA_BLOCK), BF16),
            jax.ShapeDtypeStruct((t, D_MODEL), BF16),
            jax.ShapeDtypeStruct((t // MOBA_BLOCK, D_MODEL, MOBA_BLOCK), BF16),
            jax.ShapeDtypeStruct((t // MOBA_BLOCK, 1, D_MODEL), F32),
        ],
        scratch_shapes=_weight_scratch(w),
        compiler_params=pltpu.CompilerParams(
            dimension_semantics=("arbitrary",), vmem_limit_bytes=VMEM_LIMIT),
        name="moba_qkv",
    )(x, g[0], cos, sin, w[0])


def _attn_kernel(qt_ref, k_ref, vt_ref, km_ref, o_ref, s_ref, acc_ref, bias_ref, m_ref, l_ref, *,
                 n_blocks):
    i = pl.program_id(1)
    heads = acc_ref.shape[0]
    blk = MOBA_BLOCK

    def hs(h):
        return slice(h * HEAD_DIM, (h + 1) * HEAD_DIM)

    def scores(j, h):
        rows = pl.ds(pl.multiple_of(j * blk, blk), blk)
        return jnp.dot(k_ref[rows, hs(h)], qt_ref[0, hs(h), :], preferred_element_type=F32)

    gates = [jnp.dot(km_ref[:, hs(h)].astype(BF16), qt_ref[0, hs(h), :],
                     preferred_element_type=F32) for h in range(heads)]
    own = []
    for h in range(heads):
        own.append(scores(i, h))
        s_ref[0, h] = scores(0, h)

    row = lax.broadcasted_iota(jnp.int32, (n_blocks, blk), 0)
    valid = row < i
    for h in range(heads):
        gate = jnp.where(valid, gates[h], NEG_INF)
        bias = jnp.full((n_blocks, blk), NEG_INF, F32)
        for _ in range(MOBA_TOPK):
            top = jnp.max(gate, axis=0, keepdims=True)
            first = jnp.min(jnp.where(gate == top, row, n_blocks), axis=0, keepdims=True)
            pick = row == first
            bias = jnp.where(pick, 0.0, bias)
            gate = jnp.where(pick, -jnp.inf, gate)
        bias_ref[h] = jnp.where(valid, bias, NEG_INF)

    kpos = lax.broadcasted_iota(jnp.int32, (blk, blk), 0)
    qpos = lax.broadcasted_iota(jnp.int32, (blk, blk), 1)
    causal = kpos <= qpos
    ones_rows = jnp.ones((ONES_ROWS, blk), BF16)

    def weighted_values(j, h, p):
        vt_aug = jnp.concatenate([vt_ref[j, hs(h), :], ones_rows], axis=0)
        out = jnp.dot(vt_aug, p.astype(BF16), preferred_element_type=F32)
        return out[:HEAD_DIM], out[HEAD_DIM:HEAD_DIM + 1]

    ms, ls = [], []
    for h in range(heads):
        s = jnp.where(causal, own[h], NEG_INF)
        m = jnp.max(s, axis=0, keepdims=True)
        pv, p_sum = weighted_values(i, h, jnp.exp2(s - m))
        acc_ref[h] = pv
        ls.append(p_sum)
        ms.append(m)

    def past_block(h, slot, j, m_old, l_old):
        s = s_ref[slot, h]
        bias = bias_ref[h, pl.ds(j, 1), :]
        m_new = jnp.maximum(m_old, jnp.max(s, axis=0, keepdims=True) + bias)
        alpha = jnp.exp2(m_old - m_new)
        pv, p_sum = weighted_values(j, h, jnp.exp2(s - (m_new - bias)))
        acc_ref[h] = alpha * acc_ref[h] + pv
        return m_new, alpha * l_old + p_sum

    def block_pair(u, carry):
        m_all, l_all = carry
        j0 = 2 * u
        j_next = jnp.minimum(j0 + 2, i - 1)
        ms, ls = [], []
        for h in range(heads):
            m, l = past_block(h, 0, j0, m_all[h:h + 1, :], l_all[h:h + 1, :])
            s_ref[1, h] = scores(j0 + 1, h)
            ms.append(m)
            ls.append(l)
        for h in range(heads):
            ms[h], ls[h] = past_block(h, 1, j0 + 1, ms[h], ls[h])
            s_ref[0, h] = scores(j_next, h)
        return jnp.concatenate(ms, axis=0), jnp.concatenate(ls, axis=0)

    m_all, l_all = lax.fori_loop(
        0, lax.shift_right_logical(i, 1), block_pair,
        (jnp.concatenate(ms, axis=0), jnp.concatenate(ls, axis=0)))
    m_ref[...] = m_all
    l_ref[...] = l_all

    @pl.when((i & 1) == 1)
    def _():
        for h in range(heads):
            m, l = past_block(h, 0, i - 1, m_ref[h:h + 1, :], l_ref[h:h + 1, :])
            m_ref[h:h + 1, :] = m
            l_ref[h:h + 1, :] = l

    for h in range(heads):
        out_t = acc_ref[h] * (1.0 / l_ref[h:h + 1, :])
        o_ref[:, hs(h)] = out_t.T.astype(BF16)


def _attn_call(qt, k, vt, km, batch, seq):
    n_blocks = seq // MOBA_BLOCK
    return pl.pallas_call(
        functools.partial(_attn_kernel, n_blocks=n_blocks),
        grid=(batch, n_blocks),
        in_specs=[
            pl.BlockSpec((1, D_MODEL, MOBA_BLOCK), lambda b, i: (b * n_blocks + i, 0, 0)),
            pl.BlockSpec((seq, D_MODEL), lambda b, i: (b, 0)),
            pl.BlockSpec((n_blocks, D_MODEL, MOBA_BLOCK), lambda b, i: (b, 0, 0)),
            pl.BlockSpec((n_blocks, D_MODEL), lambda b, i: (b, 0)),
        ],
        out_specs=pl.BlockSpec((MOBA_BLOCK, D_MODEL), lambda b, i: (b * n_blocks + i, 0)),
        out_shape=jax.ShapeDtypeStruct((batch * seq, D_MODEL), BF16),
        scratch_shapes=[
            pltpu.VMEM((2, N_HEADS, MOBA_BLOCK, MOBA_BLOCK), F32),
            pltpu.VMEM((N_HEADS, HEAD_DIM, MOBA_BLOCK), F32),
            pltpu.VMEM((N_HEADS, n_blocks, MOBA_BLOCK), F32),
            pltpu.VMEM((N_HEADS, MOBA_BLOCK), F32),
            pltpu.VMEM((N_HEADS, MOBA_BLOCK), F32),
        ],
        compiler_params=pltpu.CompilerParams(
            dimension_semantics=("arbitrary", "arbitrary"), vmem_limit_bytes=VMEM_LIMIT),
        name="moba_attention",
    )(qt, k, vt, km)


def _attn_mlp_kernel(x_ref, a_ref, g_ref, *rest, final):
    if final:
        gf_ref, wo_ref, wup_ref, wdn_ref, o_ref = rest
    else:
        wo_ref, wup_ref, wdn_ref, o_ref = rest
    x1 = x_ref[...] + jnp.dot(a_ref[...], wo_ref[...], preferred_element_type=F32)
    out = _mlp_residual(x1, g_ref, wup_ref, wdn_ref)
    if final:
        out = _rms(out, gf_ref[...])
    o_ref[...] = out


def _attn_mlp_call(x, a, wo, g, wup, wdn, g_final=None):
    t = x.shape[0]
    tm = ROW_TILE
    final = g_final is not None
    in_specs = [
        pl.BlockSpec((tm, D_MODEL), lambda i: (i, 0)),
        pl.BlockSpec((tm, D_MODEL), lambda i: (i, 0)),
        _resident(*g),
    ]
    args = [x, a, g[0]]
    if final:
        in_specs.append(_resident(*g_final))
        args.append(g_final[0])
    weights = (wo, wup, wdn)
    return pl.pallas_call(
        _with_weights(functools.partial(_attn_mlp_kernel, final=final), n_in=len(args),
                      layers=[w[1] for w in weights], n_out=1),
        grid=(t // tm,),
        in_specs=in_specs + [ANY_SPEC] * len(weights),
        out_specs=pl.BlockSpec((tm, D_MODEL), lambda i: (i, 0)),
        out_shape=jax.ShapeDtypeStruct((t, D_MODEL), F32),
        scratch_shapes=[s for w in weights for s in _weight_scratch(w)],
        compiler_params=pltpu.CompilerParams(
            dimension_semantics=("arbitrary",), vmem_limit_bytes=VMEM_LIMIT),
        name="attn_out_mlp",
    )(*args, *[w[0] for w in weights])


def _halo_spec(tm):
    per = tm // HALO
    return pl.BlockSpec((HALO, D_MODEL), lambda i: (jnp.maximum(i * per - 1, 0), 0))


def _pool_mlp_kernel(x_ref, halo_ref, gmix_ref, pw_ref, ps_ref, g_ref, wup_ref, wdn_ref,
                     o_ref, ext_ref, *, tiles_per_seq):
    tm = x_ref.shape[0]
    tile_in_seq = pl.program_id(0) % tiles_per_seq
    x = x_ref[...]
    gmix = gmix_ref[...]
    xn = _rms(x, gmix)
    ext_ref[0:HALO, :] = jnp.where(tile_in_seq == 0, 0.0, _rms(halo_ref[...], gmix))
    ext_ref[HALO:HALO + tm, :] = xn
    pos = tile_in_seq * tm + lax.broadcasted_iota(jnp.int32, (tm, 1), 0)
    ys = []
    for g, w in enumerate(POOL_WINDOWS):
        cs = slice(g * POOL_GROUP_DIM, (g + 1) * POOL_GROUP_DIM)
        own = xn[:, cs]
        total = own
        for back in range(1, w):
            total = total + ext_ref[HALO - back:HALO - back + tm, cs]
        count = jnp.minimum(pos + 1, w).astype(F32)
        pooled = total / count - own
        ys.append(jnp.dot(pooled.astype(BF16), pw_ref[g], preferred_element_type=F32))
    x1 = x + jnp.concatenate(ys, axis=1) * ps_ref[...]
    o_ref[...] = _mlp_residual(x1, g_ref, wup_ref, wdn_ref)


def _pool_mlp_call(x, gmix, pw, ps, g, wup, wdn, seq):
    t = x.shape[0]
    tm = ROW_TILE
    weights = (wup, wdn)
    return pl.pallas_call(
        _with_weights(functools.partial(_pool_mlp_kernel, tiles_per_seq=seq // tm), n_in=6,
                      layers=[w[1] for w in weights], n_out=1),
        grid=(t // tm,),
        in_specs=[
            pl.BlockSpec((tm, D_MODEL), lambda i: (i, 0)),
            _halo_spec(tm),
            _resident(*gmix),
            _resident(*pw),
            _resident(*ps),
            _resident(*g),
        ] + [ANY_SPEC] * len(weights),
        out_specs=pl.BlockSpec((tm, D_MODEL), lambda i: (i, 0)),
        out_shape=jax.ShapeDtypeStruct((t, D_MODEL), F32),
        scratch_shapes=[s for w in weights for s in _weight_scratch(w)]
        + [pltpu.VMEM((HALO + tm, D_MODEL), F32)],
        compiler_params=pltpu.CompilerParams(
            dimension_semantics=("arbitrary",), vmem_limit_bytes=VMEM_LIMIT),
        name="pool_mlp",
    )(x, x, gmix[0], pw[0], ps[0], g[0], *[w[0] for w in weights])


def _conv_mlp_kernel(x_ref, halo_ref, gmix_ref, cw_ref, g_ref, win_ref, wout_ref, wup_ref,
                     wdn_ref, o_ref, xe_ref, z_ref, *, tiles_per_seq):
    tm = x_ref.shape[0]
    tile_in_seq = pl.program_id(0) % tiles_per_seq
    x = x_ref[...]
    gmix = gmix_ref[...]
    xe_ref[0:HALO, :] = _rms(halo_ref[...], gmix).astype(BF16)
    xe_ref[HALO:HALO + tm, :] = _rms(x, gmix).astype(BF16)
    gate_c = jnp.dot(xe_ref[...], win_ref[:, D_MODEL:2 * D_MODEL], preferred_element_type=F32)
    hid = jnp.dot(xe_ref[...], win_ref[:, 2 * D_MODEL:3 * D_MODEL], preferred_element_type=F32)
    z_ref[...] = gate_c * hid

    @pl.when(tile_in_seq == 0)
    def _():
        z_ref[0:HALO, :] = jnp.zeros((HALO, D_MODEL), F32)

    cw = cw_ref[...]
    zc = cw[CONV_WIDTH - 1:CONV_WIDTH, :] * z_ref[HALO:HALO + tm, :]
    for back in range(1, CONV_WIDTH):
        tap = cw[CONV_WIDTH - 1 - back:CONV_WIDTH - back, :]
        zc = zc + tap * z_ref[HALO - back:HALO - back + tm, :]
    gate_b = jnp.dot(xe_ref[HALO:HALO + tm, :], win_ref[:, 0:D_MODEL],
                     preferred_element_type=F32)
    y = jnp.dot((gate_b * zc).astype(BF16), wout_ref[...], preferred_element_type=F32)
    o_ref[...] = _mlp_residual(x + y, g_ref, wup_ref, wdn_ref)


def _conv_mlp_call(x, gmix, win, cw, wout, g, wup, wdn, seq):
    t = x.shape[0]
    tm = ROW_TILE
    weights = (win, wout, wup, wdn)
    return pl.pallas_call(
        _with_weights(functools.partial(_conv_mlp_kernel, tiles_per_seq=seq // tm), n_in=5,
                      layers=[w[1] for w in weights], n_out=1),
        grid=(t // tm,),
        in_specs=[
            pl.BlockSpec((tm, D_MODEL), lambda i: (i, 0)),
            _halo_spec(tm),
            _resident(*gmix),
            _resident(*cw),
            _resident(*g),
        ] + [ANY_SPEC] * len(weights),
        out_specs=pl.BlockSpec((tm, D_MODEL), lambda i: (i, 0)),
        out_shape=jax.ShapeDtypeStruct((t, D_MODEL), F32),
        scratch_shapes=[s for w in weights for s in _weight_scratch(w)] + [
            pltpu.VMEM((HALO + tm, D_MODEL), BF16),
            pltpu.VMEM((HALO + tm, D_MODEL), F32),
        ],
        compiler_params=pltpu.CompilerParams(
            dimension_semantics=("arbitrary",), vmem_limit_bytes=VMEM_LIMIT),
        name="conv_mlp",
    )(x, x, gmix[0], cw[0], g[0], *[w[0] for w in weights])


ROPE_PACK = HEAD_DIM // ROT_DIM


def _rope_table_kernel(pos_ref, freq_ref, sign_ref, cos_ref, sin_ref):
    ang = pos_ref[...].astype(F32) * freq_ref[...]
    c = jnp.cos(ang)
    s = jnp.sin(ang) * sign_ref[...]
    rotary = lax.broadcasted_iota(jnp.int32, c.shape, 1) < ROT_DIM
    for g in range(ROPE_PACK):
        shift = (HEAD_DIM - ROT_DIM * g) % HEAD_DIM
        cg = pltpu.roll(c, shift, 1) if shift else c
        sg = pltpu.roll(s, shift, 1) if shift else s
        cos_ref[g] = jnp.where(rotary, cg, 1.0)
        sin_ref[g] = jnp.where(rotary, sg, 0.0)


def _rope_tables(positions):
    n = positions.size
    rows = n // ROPE_PACK
    half = ROT_DIM // 2
    inv_freq = ROPE_THETA ** (-jnp.arange(0, ROT_DIM, 2, dtype=F32) / ROT_DIM)
    freq = jnp.tile(inv_freq, HEAD_DIM // half).reshape(1, HEAD_DIM)
    sign = jnp.tile(jnp.concatenate([-jnp.ones((half,), F32), jnp.ones((half,), F32)]),
                    ROPE_PACK).reshape(1, HEAD_DIM)
    packed = jnp.repeat(positions.reshape(ROPE_PACK, rows).T, ROT_DIM, axis=1)
    rt = ROW_TILE
    table = jax.ShapeDtypeStruct((ROPE_PACK, rows, HEAD_DIM), F32)
    row_spec = pl.BlockSpec((1, HEAD_DIM), lambda i: (0, 0))
    cos, sin = pl.pallas_call(
        _rope_table_kernel,
        grid=(rows // rt,),
        in_specs=[pl.BlockSpec((rt, HEAD_DIM), lambda i: (i, 0)), row_spec, row_spec],
        out_specs=[pl.BlockSpec((ROPE_PACK, rt, HEAD_DIM), lambda i: (0, i, 0))] * 2,
        out_shape=[table, table],
        compiler_params=pltpu.CompilerParams(dimension_semantics=("arbitrary",)),
        name="rope_tables",
    )(packed, freq, sign)
    return cos.reshape(n, HEAD_DIM), sin.reshape(n, HEAD_DIM)


def kernel(x, positions, norm_mix, norm_mlp, attn_w_qkv, attn_w_o, pool_w, pool_scale,
           conv_w_in, conv_w, conv_w_out, mlp_w_up, mlp_w_down, norm_final):
    batch, seq, d = x.shape
    depth = norm_mix.shape[0]
    assert d == D_MODEL and seq % ROW_TILE == 0 and ROW_TILE % MOBA_BLOCK == 0
    assert (depth - 1) % 3 == 0, "the final norm is fused into a MoBA layer's MLP stage"
    cos, sin = _rope_tables(positions)
    h = x.reshape(batch * seq, d)
    rows = lambda v: v.reshape(v.shape[0], 1, v.shape[-1])
    g_mix, g_mlp, p_scale = rows(norm_mix), rows(norm_mlp), rows(pool_scale)
    g_final = norm_final.reshape(1, 1, d)
    w_pool = pool_w.astype(BF16)
    i_attn = i_pool = i_conv = 0
    for i in range(depth):
        kind = i % 3
        mlp = ((g_mlp, i), (mlp_w_up, i), (mlp_w_down, i))
        if kind == 0:
            qt, k, vt, km = _qkv_call(h, (g_mix, i), (attn_w_qkv, i_attn), cos, sin)
            a = _attn_call(qt, k, vt, km.reshape(-1, d), batch, seq)
            h = _attn_mlp_call(h, a, (attn_w_o, i_attn), *mlp,
                               g_final=(g_final, 0) if i == depth - 1 else None)
            i_attn += 1
        elif kind == 1:
            h = _pool_mlp_call(h, (g_mix, i), (w_pool, i_pool), (p_scale, i_pool), *mlp, seq)
            i_pool += 1
        else:
            h = _conv_mlp_call(h, (g_mix, i), (conv_w_in, i_conv), (conv_w, i_conv),
                               (conv_w_out, i_conv), *mlp, seq)
            i_conv += 1
    return h.reshape(batch, seq, d)
```

```python
import functools
import math

import jax
import jax.numpy as jnp
from jax import lax
from jax.experimental import pallas as pl
from jax.experimental.pallas import tpu as pltpu

D_MODEL = 1024
N_HEADS = 8
HEAD_DIM = D_MODEL // N_HEADS
ROT_DIM = HEAD_DIM // 4
ROPE_THETA = 500000.0
MOBA_BLOCK = 256
MOBA_TOPK = 3
POOL_WINDOWS = (2, 4, 8, 16)
POOL_GROUP_DIM = D_MODEL // len(POOL_WINDOWS)
CONV_WIDTH = 3
D_FF = 4 * D_MODEL
NORM_EPS = 1e-6
NEG_INF = -1e30

HALO = 16
ROW_TILE = 512
FF_CHUNK = 1024
ONES_ROWS = 16
STAGE_BYTES = 512 * 1024
IN_FLIGHT_BYTES = 4 * 1024 * 1024
ANY_SPEC = pl.BlockSpec(memory_space=pl.ANY)
VMEM_LIMIT = 56 * 1024 * 1024
Q_SCALE = (1.0 / math.sqrt(HEAD_DIM)) * math.log2(math.e)

F32 = jnp.float32
BF16 = jnp.bfloat16


def _resident(stacked, layer):
    tail = stacked.shape[1:]
    index = (layer,) + (0,) * len(tail)
    return pl.BlockSpec((None,) + tail, lambda *_: index, pipeline_mode=pl.Buffered(1))


def _chunk_rows(width):
    rows = STAGE_BYTES // (4 * width)
    return rows - rows % 16


def _stage_slots(n_weights):
    return -(-IN_FLIGHT_BYTES // (n_weights * STAGE_BYTES)) + 1


def _weight_scratch(weights):
    slots = _stage_slots(len(weights))
    out = []
    for w in weights:
        rows, width = w[0].shape[1:]
        out += [pltpu.VMEM((rows, width), BF16),
                pltpu.VMEM((slots, _chunk_rows(width), width), F32),
                pltpu.SemaphoreType.DMA((slots,))]
    return out


def _fetch_weights(hbm, layers, fetched):
    plans = []
    for w_hbm, layer, (w_vmem, stage, sem) in zip(hbm, layers, fetched):
        rows = w_vmem.shape[0]
        slots, r = stage.shape[:2]
        assert rows % r == 0 and rows // r >= slots

        def chunk(c, w_hbm=w_hbm, layer=layer, stage=stage, sem=sem, slots=slots, r=r):
            slot = c % slots
            return pltpu.make_async_copy(w_hbm.at[layer, pl.ds(c * r, r), :], stage.at[slot],
                                         sem.at[slot])

        plans.append((rows // r, slots, r, chunk, w_vmem, stage))

    for _, slots, _, chunk, _, _ in plans:
        for c in range(slots - 1):
            chunk(c).start()
    trips = max(p[0] for p in plans)

    def step(c, carry):
        for n, slots, r, chunk, w_vmem, stage in plans:
            def advance(n=n, slots=slots, r=r, chunk=chunk, w_vmem=w_vmem, stage=stage):
                @pl.when(c + slots - 1 < n)
                def _():
                    chunk(c + slots - 1).start()

                chunk(c).wait()
                w_vmem[pl.ds(pl.multiple_of(c * r, r), r), :] = stage[c % slots].astype(BF16)

            if n == trips:
                advance()
            else:
                pl.when(c < n)(advance)
        return carry

    lax.fori_loop(0, trips, step, 0)


def _with_weights(body, n_in, layers, n_out):
    n_w = len(layers)

    def kernel(*refs):
        ins = refs[:n_in]
        hbm = refs[n_in:n_in + n_w]
        outs = refs[n_in + n_w:n_in + n_w + n_out]
        scratch = refs[n_in + n_w + n_out:]
        fetched = [scratch[3 * k:3 * k + 3] for k in range(n_w)]

        @pl.when(pl.program_id(0) == 0)
        def _():
            _fetch_weights(hbm, layers, fetched)

        body(*ins, *[f[0] for f in fetched], *outs, *scratch[3 * n_w:])

    return kernel


def _rms(x, g):
    ms = jnp.mean(x * x, axis=-1, keepdims=True)
    return x * lax.rsqrt(ms + NORM_EPS) * g


def _mlp_residual(x1, g_ref, wup_ref, wdn_ref):
    xn = _rms(x1, g_ref[...]).astype(BF16)
    acc = jnp.zeros_like(x1)
    for c in range(D_FF // FF_CHUNK):
        cs = slice(c * FF_CHUNK, (c + 1) * FF_CHUNK)
        h = jnp.dot(xn, wup_ref[:, cs], preferred_element_type=F32)
        h = jnp.maximum(h, 0.0)
        h = (h * h).astype(BF16)
        acc = acc + jnp.dot(h, wdn_ref[cs, :], preferred_element_type=F32)
    return x1 + acc


def _qkv_kernel(x_ref, g_ref, cos_ref, sin_ref, w_ref, qt_ref, k_ref, vt_ref, km_ref):
    tm = x_ref.shape[0]
    xn = _rms(x_ref[...], g_ref[...]).astype(BF16)
    cos = cos_ref[...]
    sin = sin_ref[...]
    lane = lax.broadcasted_iota(jnp.int32, (tm, HEAD_DIM), 1)
    first_half = lane < (ROT_DIM // 2)

    def rope(t):
        parts = []
        for h in range(N_HEADS):
            th = t[:, h * HEAD_DIM:(h + 1) * HEAD_DIM]
            partner = jnp.where(first_half,
                                pltpu.roll(th, HEAD_DIM - ROT_DIM // 2, 1),
                                pltpu.roll(th, ROT_DIM // 2, 1))
            parts.append(th * cos + partner * sin)
        return jnp.concatenate(parts, axis=1)

    q = jnp.dot(xn, w_ref[:, 0:D_MODEL], preferred_element_type=F32)
    q = rope(q) * Q_SCALE
    for b in range(tm // MOBA_BLOCK):
        qt_ref[b] = q[b * MOBA_BLOCK:(b + 1) * MOBA_BLOCK, :].T.astype(BF16)

    k = jnp.dot(xn, w_ref[:, D_MODEL:2 * D_MODEL], preferred_element_type=F32)
    k = rope(k)
    k_ref[...] = k.astype(BF16)
    for b in range(tm // MOBA_BLOCK):
        km_ref[b] = jnp.mean(k[b * MOBA_BLOCK:(b + 1) * MOBA_BLOCK, :], axis=0, keepdims=True)

    v = jnp.dot(xn, w_ref[:, 2 * D_MODEL:3 * D_MODEL], preferred_element_type=F32)
    for b in range(tm // MOBA_BLOCK):
        vt_ref[b] = v[b * MOBA_BLOCK:(b + 1) * MOBA_BLOCK, :].T.astype(BF16)


def _qkv_call(x, g, w, cos, sin):
    t = x.shape[0]
    tm = ROW_TILE
    nb = tm // MOBA_BLOCK
    return pl.pallas_call(
        _with_weights(_qkv_kernel, n_in=4, layers=(w[1],), n_out=4),
        grid=(t // tm,),
        in_specs=[
            pl.BlockSpec((tm, D_MODEL), lambda i: (i, 0)),
            _resident(*g),
            pl.BlockSpec((tm, HEAD_DIM), lambda i: (i, 0)),
            pl.BlockSpec((tm, HEAD_DIM), lambda i: (i, 0)),
            ANY_SPEC,
        ],
        out_specs=[
            pl.BlockSpec((nb, D_MODEL, MOBA_BLOCK), lambda i: (i, 0, 0)),
            pl.BlockSpec((tm, D_MODEL), lambda i: (i, 0)),
            pl.BlockSpec((nb, D_MODEL, MOBA_BLOCK), lambda i: (i, 0, 0)),
            pl.BlockSpec((nb, 1, D_MODEL), lambda i: (i, 0, 0)),
        ],
        out_shape=[
            jax.ShapeDtypeStruct((t // MOBA_BLOCK, D_MODEL, MOBA_BLOCK), BF16),
            jax.ShapeDtypeStruct((t, D_MODEL), BF16),
            jax.ShapeDtypeStruct((t // MOBA_BLOCK, D_MODEL, MOBA_BLOCK), BF16),
            jax.ShapeDtypeStruct((t // MOBA_BLOCK, 1, D_MODEL), F32),
        ],
        scratch_shapes=_weight_scratch((w,)),
        compiler_params=pltpu.CompilerParams(
            dimension_semantics=("arbitrary",), vmem_limit_bytes=VMEM_LIMIT),
        name="moba_qkv",
    )(x, g[0], cos, sin, w[0])


def _attn_kernel(qt_ref, k_ref, vt_ref, km_ref, o_ref, s_ref, acc_ref, bias_ref, m_ref, l_ref, *,
                 n_blocks):
    i = pl.program_id(1)
    heads = acc_ref.shape[0]
    blk = MOBA_BLOCK

    def hs(h):
        return slice(h * HEAD_DIM, (h + 1) * HEAD_DIM)

    def scores(j, h):
        rows = pl.ds(pl.multiple_of(j * blk, blk), blk)
        return jnp.dot(k_ref[rows, hs(h)], qt_ref[0, hs(h), :], preferred_element_type=F32)

    gates = [jnp.dot(km_ref[:, hs(h)].astype(BF16), qt_ref[0, hs(h), :],
                     preferred_element_type=F32) for h in range(heads)]
    own = []
    for h in range(heads):
        own.append(scores(i, h))
        s_ref[0, h] = scores(0, h)

    row = lax.broadcasted_iota(jnp.int32, (n_blocks, blk), 0)
    valid = row < i
    for h in range(heads):
        gate = jnp.where(valid, gates[h], NEG_INF)
        bias = jnp.full((n_blocks, blk), NEG_INF, F32)
        for _ in range(MOBA_TOPK):
            top = jnp.max(gate, axis=0, keepdims=True)
            first = jnp.min(jnp.where(gate == top, row, n_blocks), axis=0, keepdims=True)
            pick = row == first
            bias = jnp.where(pick, 0.0, bias)
            gate = jnp.where(pick, -jnp.inf, gate)
        bias_ref[h] = jnp.where(valid, bias, NEG_INF)

    kpos = lax.broadcasted_iota(jnp.int32, (blk, blk), 0)
    qpos = lax.broadcasted_iota(jnp.int32, (blk, blk), 1)
    causal = kpos <= qpos
    ones_rows = jnp.ones((ONES_ROWS, blk), BF16)

    def weighted_values(j, h, p):
        vt_aug = jnp.concatenate([vt_ref[j, hs(h), :], ones_rows], axis=0)
        out = jnp.dot(vt_aug, p.astype(BF16), preferred_element_type=F32)
        return out[:HEAD_DIM], out[HEAD_DIM:HEAD_DIM + 1]

    ms, ls = [], []
    for h in range(heads):
        s = jnp.where(causal, own[h], NEG_INF)
        m = jnp.max(s, axis=0, keepdims=True)
        pv, p_sum = weighted_values(i, h, jnp.exp2(s - m))
        acc_ref[h] = pv
        ls.append(p_sum)
        ms.append(m)

    def past_block(h, slot, j, m_old, l_old):
        s = s_ref[slot, h]
        bias = bias_ref[h, pl.ds(j, 1), :]
        m_new = jnp.maximum(m_old, jnp.max(s, axis=0, keepdims=True) + bias)
        alpha = jnp.exp2(m_old - m_new)
        pv, p_sum = weighted_values(j, h, jnp.exp2(s - (m_new - bias)))
        acc_ref[h] = alpha * acc_ref[h] + pv
        return m_new, alpha * l_old + p_sum

    def block_pair(u, carry):
        m_all, l_all = carry
        j0 = 2 * u
        j_next = jnp.minimum(j0 + 2, i - 1)
        ms, ls = [], []
        for h in range(heads):
            m, l = past_block(h, 0, j0, m_all[h:h + 1, :], l_all[h:h + 1, :])
            s_ref[1, h] = scores(j0 + 1, h)
            ms.append(m)
            ls.append(l)
        for h in range(heads):
            ms[h], ls[h] = past_block(h, 1, j0 + 1, ms[h], ls[h])
            s_ref[0, h] = scores(j_next, h)
        return jnp.concatenate(ms, axis=0), jnp.concatenate(ls, axis=0)

    m_all, l_all = lax.fori_loop(
        0, lax.shift_right_logical(i, 1), block_pair,
        (jnp.concatenate(ms, axis=0), jnp.concatenate(ls, axis=0)))
    m_ref[...] = m_all
    l_ref[...] = l_all

    @pl.when((i & 1) == 1)
    def _():
        for h in range(heads):
            m, l = past_block(h, 0, i - 1, m_ref[h:h + 1, :], l_ref[h:h + 1, :])
            m_ref[h:h + 1, :] = m
            l_ref[h:h + 1, :] = l

    for h in range(heads):
        out_t = acc_ref[h] * (1.0 / l_ref[h:h + 1, :])
        o_ref[:, hs(h)] = out_t.T.astype(BF16)


def _attn_call(qt, k, vt, km, batch, seq):
    n_blocks = seq // MOBA_BLOCK
    return pl.pallas_call(
        functools.partial(_attn_kernel, n_blocks=n_blocks),
        grid=(batch, n_blocks),
        in_specs=[
            pl.BlockSpec((1, D_MODEL, MOBA_BLOCK), lambda b, i: (b * n_blocks + i, 0, 0)),
            pl.BlockSpec((seq, D_MODEL), lambda b, i: (b, 0)),
            pl.BlockSpec((n_blocks, D_MODEL, MOBA_BLOCK), lambda b, i: (b, 0, 0)),
            pl.BlockSpec((n_blocks, D_MODEL), lambda b, i: (b, 0)),
        ],
        out_specs=pl.BlockSpec((MOBA_BLOCK, D_MODEL), lambda b, i: (b * n_blocks + i, 0)),
        out_shape=jax.ShapeDtypeStruct((batch * seq, D_MODEL), BF16),
        scratch_shapes=[
            pltpu.VMEM((2, N_HEADS, MOBA_BLOCK, MOBA_BLOCK), F32),
            pltpu.VMEM((N_HEADS, HEAD_DIM, MOBA_BLOCK), F32),
            pltpu.VMEM((N_HEADS, n_blocks, MOBA_BLOCK), F32),
            pltpu.VMEM((N_HEADS, MOBA_BLOCK), F32),
            pltpu.VMEM((N_HEADS, MOBA_BLOCK), F32),
        ],
        compiler_params=pltpu.CompilerParams(
            dimension_semantics=("arbitrary", "arbitrary"), vmem_limit_bytes=VMEM_LIMIT),
        name="moba_attention",
    )(qt, k, vt, km)


def _attn_mlp_kernel(x_ref, a_ref, g_ref, *rest, final):
    if final:
        gf_ref, wo_ref, wup_ref, wdn_ref, o_ref = rest
    else:
        wo_ref, wup_ref, wdn_ref, o_ref = rest
    x1 = x_ref[...] + jnp.dot(a_ref[...], wo_ref[...], preferred_element_type=F32)
    out = _mlp_residual(x1, g_ref, wup_ref, wdn_ref)
    if final:
        out = _rms(out, gf_ref[...])
    o_ref[...] = out


def _attn_mlp_call(x, a, wo, g, wup, wdn, g_final=None):
    t = x.shape[0]
    tm = ROW_TILE
    final = g_final is not None
    in_specs = [
        pl.BlockSpec((tm, D_MODEL), lambda i: (i, 0)),
        pl.BlockSpec((tm, D_MODEL), lambda i: (i, 0)),
        _resident(*g),
    ]
    args = [x, a, g[0]]
    if final:
        in_specs.append(_resident(*g_final))
        args.append(g_final[0])
    weights = (wo, wup, wdn)
    return pl.pallas_call(
        _with_weights(functools.partial(_attn_mlp_kernel, final=final), n_in=len(args),
                      layers=[w[1] for w in weights], n_out=1),
        grid=(t // tm,),
        in_specs=in_specs + [ANY_SPEC] * len(weights),
        out_specs=pl.BlockSpec((tm, D_MODEL), lambda i: (i, 0)),
        out_shape=jax.ShapeDtypeStruct((t, D_MODEL), F32),
        scratch_shapes=_weight_scratch(weights),
        compiler_params=pltpu.CompilerParams(
            dimension_semantics=("arbitrary",), vmem_limit_bytes=VMEM_LIMIT),
        name="attn_out_mlp",
    )(*args, *[w[0] for w in weights])


def _halo_spec(tm):
    per = tm // HALO
    return pl.BlockSpec((HALO, D_MODEL), lambda i: (jnp.maximum(i * per - 1, 0), 0))


def _pool_mlp_kernel(x_ref, halo_ref, gmix_ref, pw_ref, ps_ref, g_ref, wup_ref, wdn_ref,
                     o_ref, ext_ref, *, tiles_per_seq):
    tm = x_ref.shape[0]
    tile_in_seq = pl.program_id(0) % tiles_per_seq
    x = x_ref[...]
    gmix = gmix_ref[...]
    xn = _rms(x, gmix)
    ext_ref[0:HALO, :] = jnp.where(tile_in_seq == 0, 0.0, _rms(halo_ref[...], gmix))
    ext_ref[HALO:HALO + tm, :] = xn
    pos = tile_in_seq * tm + lax.broadcasted_iota(jnp.int32, (tm, 1), 0)
    ys = []
    for g, w in enumerate(POOL_WINDOWS):
        cs = slice(g * POOL_GROUP_DIM, (g + 1) * POOL_GROUP_DIM)
        own = xn[:, cs]
        total = own
        for back in range(1, w):
            total = total + ext_ref[HALO - back:HALO - back + tm, cs]
        count = jnp.minimum(pos + 1, w).astype(F32)
        pooled = total / count - own
        ys.append(jnp.dot(pooled.astype(BF16), pw_ref[g], preferred_element_type=F32))
    x1 = x + jnp.concatenate(ys, axis=1) * ps_ref[...]
    o_ref[...] = _mlp_residual(x1, g_ref, wup_ref, wdn_ref)


def _pool_mlp_call(x, gmix, pw, ps, g, wup, wdn, seq):
    t = x.shape[0]
    tm = ROW_TILE
    weights = (wup, wdn)
    return pl.pallas_call(
        _with_weights(functools.partial(_pool_mlp_kernel, tiles_per_seq=seq // tm), n_in=6,
                      layers=[w[1] for w in weights], n_out=1),
        grid=(t // tm,),
        in_specs=[
            pl.BlockSpec((tm, D_MODEL), lambda i: (i, 0)),
            _halo_spec(tm),
            _resident(*gmix),
            _resident(*pw),
            _resident(*ps),
            _resident(*g),
        ] + [ANY_SPEC] * len(weights),
        out_specs=pl.BlockSpec((tm, D_MODEL), lambda i: (i, 0)),
        out_shape=jax.ShapeDtypeStruct((t, D_MODEL), F32),
        scratch_shapes=_weight_scratch(weights) + [pltpu.VMEM((HALO + tm, D_MODEL), F32)],
        compiler_params=pltpu.CompilerParams(
            dimension_semantics=("arbitrary",), vmem_limit_bytes=VMEM_LIMIT),
        name="pool_mlp",
    )(x, x, gmix[0], pw[0], ps[0], g[0], *[w[0] for w in weights])


def _conv_mlp_kernel(x_ref, halo_ref, gmix_ref, cw_ref, g_ref, win_ref, wout_ref, wup_ref,
                     wdn_ref, o_ref, xe_ref, z_ref, *, tiles_per_seq):
    tm = x_ref.shape[0]
    tile_in_seq = pl.program_id(0) % tiles_per_seq
    x = x_ref[...]
    gmix = gmix_ref[...]
    xe_ref[0:HALO, :] = _rms(halo_ref[...], gmix).astype(BF16)
    xe_ref[HALO:HALO + tm, :] = _rms(x, gmix).astype(BF16)
    gate_c = jnp.dot(xe_ref[...], win_ref[:, D_MODEL:2 * D_MODEL], preferred_element_type=F32)
    hid = jnp.dot(xe_ref[...], win_ref[:, 2 * D_MODEL:3 * D_MODEL], preferred_element_type=F32)
    z_ref[...] = gate_c * hid

    @pl.when(tile_in_seq == 0)
    def _():
        z_ref[0:HALO, :] = jnp.zeros((HALO, D_MODEL), F32)

    cw = cw_ref[...]
    zc = cw[CONV_WIDTH - 1:CONV_WIDTH, :] * z_ref[HALO:HALO + tm, :]
    for back in range(1, CONV_WIDTH):
        tap = cw[CONV_WIDTH - 1 - back:CONV_WIDTH - back, :]
        zc = zc + tap * z_ref[HALO - back:HALO - back + tm, :]
    gate_b = jnp.dot(xe_ref[HALO:HALO + tm, :], win_ref[:, 0:D_MODEL],
                     preferred_element_type=F32)
    y = jnp.dot((gate_b * zc).astype(BF16), wout_ref[...], preferred_element_type=F32)
    o_ref[...] = _mlp_residual(x + y, g_ref, wup_ref, wdn_ref)


def _conv_mlp_call(x, gmix, win, cw, wout, g, wup, wdn, seq):
    t = x.shape[0]
    tm = ROW_TILE
    weights = (win, wout, wup, wdn)
    return pl.pallas_call(
        _with_weights(functools.partial(_conv_mlp_kernel, tiles_per_seq=seq // tm), n_in=5,
                      layers=[w[1] for w in weights], n_out=1),
        grid=(t // tm,),
        in_specs=[
            pl.BlockSpec((tm, D_MODEL), lambda i: (i, 0)),
            _halo_spec(tm),
            _resident(*gmix),
            _resident(*cw),
            _resident(*g),
        ] + [ANY_SPEC] * len(weights),
        out_specs=pl.BlockSpec((tm, D_MODEL), lambda i: (i, 0)),
        out_shape=jax.ShapeDtypeStruct((t, D_MODEL), F32),
        scratch_shapes=_weight_scratch(weights) + [
            pltpu.VMEM((HALO + tm, D_MODEL), BF16),
            pltpu.VMEM((HALO + tm, D_MODEL), F32),
        ],
        compiler_params=pltpu.CompilerParams(
            dimension_semantics=("arbitrary",), vmem_limit_bytes=VMEM_LIMIT),
        name="conv_mlp",
    )(x, x, gmix[0], cw[0], g[0], *[w[0] for w in weights])


ROPE_PACK = HEAD_DIM // ROT_DIM


def _rope_table_kernel(pos_ref, freq_ref, sign_ref, cos_ref, sin_ref):
    ang = pos_ref[...].astype(F32) * freq_ref[...]
    c = jnp.cos(ang)
    s = jnp.sin(ang) * sign_ref[...]
    rotary = lax.broadcasted_iota(jnp.int32, c.shape, 1) < ROT_DIM
    for g in range(ROPE_PACK):
        shift = (HEAD_DIM - ROT_DIM * g) % HEAD_DIM
        cg = pltpu.roll(c, shift, 1) if shift else c
        sg = pltpu.roll(s, shift, 1) if shift else s
        cos_ref[g] = jnp.where(rotary, cg, 1.0)
        sin_ref[g] = jnp.where(rotary, sg, 0.0)


def _rope_tables(positions):
    n = positions.size
    rows = n // ROPE_PACK
    half = ROT_DIM // 2
    inv_freq = ROPE_THETA ** (-jnp.arange(0, ROT_DIM, 2, dtype=F32) / ROT_DIM)
    freq = jnp.tile(inv_freq, HEAD_DIM // half).reshape(1, HEAD_DIM)
    sign = jnp.tile(jnp.concatenate([-jnp.ones((half,), F32), jnp.ones((half,), F32)]),
                    ROPE_PACK).reshape(1, HEAD_DIM)
    packed = jnp.repeat(positions.reshape(ROPE_PACK, rows).T, ROT_DIM, axis=1)
    rt = ROW_TILE
    table = jax.ShapeDtypeStruct((ROPE_PACK, rows, HEAD_DIM), F32)
    row_spec = pl.BlockSpec((1, HEAD_DIM), lambda i: (0, 0))
    cos, sin = pl.pallas_call(
        _rope_table_kernel,
        grid=(rows // rt,),
        in_specs=[pl.BlockSpec((rt, HEAD_DIM), lambda i: (i, 0)), row_spec, row_spec],
        out_specs=[pl.BlockSpec((ROPE_PACK, rt, HEAD_DIM), lambda i: (0, i, 0))] * 2,
        out_shape=[table, table],
        compiler_params=pltpu.CompilerParams(dimension_semantics=("arbitrary",)),
        name="rope_tables",
    )(packed, freq, sign)
    return cos.reshape(n, HEAD_DIM), sin.reshape(n, HEAD_DIM)


def kernel(x, positions, norm_mix, norm_mlp, attn_w_qkv, attn_w_o, pool_w, pool_scale,
           conv_w_in, conv_w, conv_w_out, mlp_w_up, mlp_w_down, norm_final):
    batch, seq, d = x.shape
    depth = norm_mix.shape[0]
    assert d == D_MODEL and seq % ROW_TILE == 0 and ROW_TILE % MOBA_BLOCK == 0
    assert (depth - 1) % 3 == 0, "the final norm is fused into a MoBA layer's MLP stage"
    cos, sin = _rope_tables(positions)
    h = x.reshape(batch * seq, d)
    rows = lambda v: v.reshape(v.shape[0], 1, v.shape[-1])
    g_mix, g_mlp, p_scale = rows(norm_mix), rows(norm_mlp), rows(pool_scale)
    g_final = norm_final.reshape(1, 1, d)
    w_pool = pool_w.astype(BF16)
    i_attn = i_pool = i_conv = 0
    for i in range(depth):
        kind = i % 3
        mlp = ((g_mlp, i), (mlp_w_up, i), (mlp_w_down, i))
        if kind == 0:
            qt, k, vt, km = _qkv_call(h, (g_mix, i), (attn_w_qkv, i_attn), cos, sin)
            a = _attn_call(qt, k, vt, km.reshape(-1, d), batch, seq)
            h = _attn_mlp_call(h, a, (attn_w_o, i_attn), *mlp,
                               g_final=(g_final, 0) if i == depth - 1 else None)
            i_attn += 1
        elif kind == 1:
            h = _pool_mlp_call(h, (g_mix, i), (w_pool, i_pool), (p_scale, i_pool), *mlp, seq)
            i_pool += 1
        else:
            h = _conv_mlp_call(h, (g_mix, i), (conv_w_in, i_conv), (conv_w, i_conv),
                               (conv_w_out, i_conv), *mlp, seq)
            i_conv += 1
    return h.reshape(batch, seq, d)
```

```python
import functools
import math

import jax
import jax.numpy as jnp
from jax import lax
from jax.experimental import pallas as pl
from jax.experimental.pallas import tpu as pltpu

D_MODEL = 1024
N_HEADS = 8
HEAD_DIM = D_MODEL // N_HEADS
ROT_DIM = HEAD_DIM // 4
ROPE_THETA = 500000.0
MOBA_BLOCK = 256
MOBA_TOPK = 3
POOL_WINDOWS = (2, 4, 8, 16)
POOL_GROUP_DIM = D_MODEL // len(POOL_WINDOWS)
CONV_WIDTH = 3
D_FF = 4 * D_MODEL
NORM_EPS = 1e-6
NEG_INF = -1e30

HALO = 16
ROW_TILE = 512
FF_CHUNK = 1024
ONES_ROWS = 16
GROUP_SHIFT = 2
GROUP = 1 << GROUP_SHIFT
STAGE_BYTES = 512 * 1024
IN_FLIGHT_BYTES = 4 * 1024 * 1024
ANY_SPEC = pl.BlockSpec(memory_space=pl.ANY)
VMEM_LIMIT = 56 * 1024 * 1024
Q_SCALE = (1.0 / math.sqrt(HEAD_DIM)) * math.log2(math.e)

F32 = jnp.float32
BF16 = jnp.bfloat16


def _resident(stacked, layer):
    tail = stacked.shape[1:]
    index = (layer,) + (0,) * len(tail)
    return pl.BlockSpec((None,) + tail, lambda *_: index, pipeline_mode=pl.Buffered(1))


def _chunk_rows(width):
    rows = STAGE_BYTES // (4 * width)
    return rows - rows % 16


def _stage_slots(n_weights):
    return -(-IN_FLIGHT_BYTES // (n_weights * STAGE_BYTES)) + 1


def _weight_scratch(weights):
    slots = _stage_slots(len(weights))
    out = []
    for w in weights:
        rows, width = w[0].shape[1:]
        out += [pltpu.VMEM((rows, width), BF16),
                pltpu.VMEM((slots, _chunk_rows(width), width), F32),
                pltpu.SemaphoreType.DMA((slots,))]
    return out


def _fetch_weights(hbm, layers, fetched):
    plans = []
    for w_hbm, layer, (w_vmem, stage, sem) in zip(hbm, layers, fetched):
        rows = w_vmem.shape[0]
        slots, r = stage.shape[:2]
        assert rows % r == 0 and rows // r >= slots

        def chunk(c, w_hbm=w_hbm, layer=layer, stage=stage, sem=sem, slots=slots, r=r):
            slot = c % slots
            return pltpu.make_async_copy(w_hbm.at[layer, pl.ds(c * r, r), :], stage.at[slot],
                                         sem.at[slot])

        plans.append((rows // r, slots, r, chunk, w_vmem, stage))

    for _, slots, _, chunk, _, _ in plans:
        for c in range(slots - 1):
            chunk(c).start()
    trips = max(p[0] for p in plans)

    def step(c, carry):
        for n, slots, r, chunk, w_vmem, stage in plans:
            def advance(n=n, slots=slots, r=r, chunk=chunk, w_vmem=w_vmem, stage=stage):
                @pl.when(c + slots - 1 < n)
                def _():
                    chunk(c + slots - 1).start()

                chunk(c).wait()
                w_vmem[pl.ds(pl.multiple_of(c * r, r), r), :] = stage[c % slots].astype(BF16)

            if n == trips:
                advance()
            else:
                pl.when(c < n)(advance)
        return carry

    lax.fori_loop(0, trips, step, 0)


def _with_weights(body, n_in, layers, n_out):
    n_w = len(layers)

    def kernel(*refs):
        ins = refs[:n_in]
        hbm = refs[n_in:n_in + n_w]
        outs = refs[n_in + n_w:n_in + n_w + n_out]
        scratch = refs[n_in + n_w + n_out:]
        fetched = [scratch[3 * k:3 * k + 3] for k in range(n_w)]

        @pl.when(pl.program_id(0) == 0)
        def _():
            _fetch_weights(hbm, layers, fetched)

        body(*ins, *[f[0] for f in fetched], *outs, *scratch[3 * n_w:])

    return kernel


def _rms(x, g):
    ms = jnp.mean(x * x, axis=-1, keepdims=True)
    return x * lax.rsqrt(ms + NORM_EPS) * g


def _mlp_residual(x1, g_ref, wup_ref, wdn_ref):
    xn = _rms(x1, g_ref[...]).astype(BF16)
    acc = jnp.zeros_like(x1)
    for c in range(D_FF // FF_CHUNK):
        cs = slice(c * FF_CHUNK, (c + 1) * FF_CHUNK)
        h = jnp.dot(xn, wup_ref[:, cs], preferred_element_type=F32)
        h = jnp.maximum(h, 0.0)
        h = (h * h).astype(BF16)
        acc = acc + jnp.dot(h, wdn_ref[cs, :], preferred_element_type=F32)
    return x1 + acc


def _qkv_kernel(x_ref, g_ref, cos_ref, sin_ref, w_ref, qt_ref, k_ref, vt_ref, km_ref):
    tm = x_ref.shape[0]
    xn = _rms(x_ref[...], g_ref[...]).astype(BF16)
    cos = cos_ref[...]
    sin = sin_ref[...]
    lane = lax.broadcasted_iota(jnp.int32, (tm, HEAD_DIM), 1)
    first_half = lane < (ROT_DIM // 2)

    def rope(t):
        parts = []
        for h in range(N_HEADS):
            th = t[:, h * HEAD_DIM:(h + 1) * HEAD_DIM]
            partner = jnp.where(first_half,
                                pltpu.roll(th, HEAD_DIM - ROT_DIM // 2, 1),
                                pltpu.roll(th, ROT_DIM // 2, 1))
            parts.append(th * cos + partner * sin)
        return jnp.concatenate(parts, axis=1)

    q = jnp.dot(xn, w_ref[:, 0:D_MODEL], preferred_element_type=F32)
    q = rope(q) * Q_SCALE
    for b in range(tm // MOBA_BLOCK):
        qt_ref[b] = q[b * MOBA_BLOCK:(b + 1) * MOBA_BLOCK, :].T.astype(BF16)

    k = jnp.dot(xn, w_ref[:, D_MODEL:2 * D_MODEL], preferred_element_type=F32)
    k = rope(k)
    k_ref[...] = k.astype(BF16)
    for b in range(tm // MOBA_BLOCK):
        km_ref[b] = jnp.mean(k[b * MOBA_BLOCK:(b + 1) * MOBA_BLOCK, :], axis=0, keepdims=True)

    v = jnp.dot(xn, w_ref[:, 2 * D_MODEL:3 * D_MODEL], preferred_element_type=F32)
    for b in range(tm // MOBA_BLOCK):
        vt_ref[b] = v[b * MOBA_BLOCK:(b + 1) * MOBA_BLOCK, :].T.astype(BF16)


def _qkv_call(x, g, w, cos, sin):
    t = x.shape[0]
    tm = ROW_TILE
    nb = tm // MOBA_BLOCK
    return pl.pallas_call(
        _with_weights(_qkv_kernel, n_in=4, layers=(w[1],), n_out=4),
        grid=(t // tm,),
        in_specs=[
            pl.BlockSpec((tm, D_MODEL), lambda i: (i, 0)),
            _resident(*g),
            pl.BlockSpec((tm, HEAD_DIM), lambda i: (i, 0)),
            pl.BlockSpec((tm, HEAD_DIM), lambda i: (i, 0)),
            ANY_SPEC,
        ],
        out_specs=[
            pl.BlockSpec((nb, D_MODEL, MOBA_BLOCK), lambda i: (i, 0, 0)),
            pl.BlockSpec((tm, D_MODEL), lambda i: (i, 0)),
            pl.BlockSpec((nb, D_MODEL, MOBA_BLOCK), lambda i: (i, 0, 0)),
            pl.BlockSpec((nb, 1, D_MODEL), lambda i: (i, 0, 0)),
        ],
        out_shape=[
            jax.ShapeDtypeStruct((t // MOBA_BLOCK, D_MODEL, MOBA_BLOCK), BF16),
            jax.ShapeDtypeStruct((t, D_MODEL), BF16),
            jax.ShapeDtypeStruct((t // MOBA_BLOCK, D_MODEL, MOBA_BLOCK), BF16),
            jax.ShapeDtypeStruct((t // MOBA_BLOCK, 1, D_MODEL), F32),
        ],
        scratch_shapes=_weight_scratch((w,)),
        compiler_params=pltpu.CompilerParams(
            dimension_semantics=("arbitrary",), vmem_limit_bytes=VMEM_LIMIT),
        name="moba_qkv",
    )(x, g[0], cos, sin, w[0])


def _attn_kernel(qt_ref, k_ref, vt_ref, km_ref, o_ref, s_ref, acc_ref, bias_ref, m_ref, l_ref, *,
                 n_blocks):
    i = pl.program_id(1)
    heads = acc_ref.shape[0]
    blk = MOBA_BLOCK

    def hs(h):
        return slice(h * HEAD_DIM, (h + 1) * HEAD_DIM)

    def scores(j, h):
        rows = pl.ds(pl.multiple_of(j * blk, blk), blk)
        return jnp.dot(k_ref[rows, hs(h)], qt_ref[0, hs(h), :], preferred_element_type=F32)

    gates = [jnp.dot(km_ref[:, hs(h)].astype(BF16), qt_ref[0, hs(h), :],
                     preferred_element_type=F32) for h in range(heads)]
    own = []
    for h in range(heads):
        own.append(scores(i, h))
        s_ref[0, h] = scores(0, h)

    row = lax.broadcasted_iota(jnp.int32, (n_blocks, blk), 0)
    valid = row < i
    for h in range(heads):
        gate = jnp.where(valid, gates[h], NEG_INF)
        bias = jnp.full((n_blocks, blk), NEG_INF, F32)
        for _ in range(MOBA_TOPK):
            top = jnp.max(gate, axis=0, keepdims=True)
            first = jnp.min(jnp.where(gate == top, row, n_blocks), axis=0, keepdims=True)
            pick = row == first
            bias = jnp.where(pick, 0.0, bias)
            gate = jnp.where(pick, -jnp.inf, gate)
        bias_ref[h] = jnp.where(valid, bias, NEG_INF)

    kpos = lax.broadcasted_iota(jnp.int32, (blk, blk), 0)
    qpos = lax.broadcasted_iota(jnp.int32, (blk, blk), 1)
    causal = kpos <= qpos
    ones_rows = jnp.ones((ONES_ROWS, blk), BF16)

    def weighted_values(j, h, p):
        vt_aug = jnp.concatenate([vt_ref[j, hs(h), :], ones_rows], axis=0)
        out = jnp.dot(vt_aug, p.astype(BF16), preferred_element_type=F32)
        return out[:HEAD_DIM], out[HEAD_DIM:HEAD_DIM + 1]

    ms, ls = [], []
    for h in range(heads):
        s = jnp.where(causal, own[h], NEG_INF)
        m = jnp.max(s, axis=0, keepdims=True)
        pv, p_sum = weighted_values(i, h, jnp.exp2(s - m))
        acc_ref[h] = pv
        ls.append(p_sum)
        ms.append(m)

    def past_block(h, slot, j, m_old, l_old):
        s = s_ref[slot, h]
        bias = bias_ref[h, pl.ds(j, 1), :]
        m_new = jnp.maximum(m_old, jnp.max(s, axis=0, keepdims=True) + bias)
        alpha = jnp.exp2(m_old - m_new)
        pv, p_sum = weighted_values(j, h, jnp.exp2(s - (m_new - bias)))
        acc_ref[h] = alpha * acc_ref[h] + pv
        return m_new, alpha * l_old + p_sum

    def block_group(j0, count, m_all, l_all):
        ms = [m_all[h:h + 1, :] for h in range(heads)]
        ls = [l_all[h:h + 1, :] for h in range(heads)]
        for b in range(count):
            slot = b % 2
            j_next = j0 + b + 1 if b + 1 < count else jnp.minimum(j0 + count, i - 1)
            for h in range(heads):
                ms[h], ls[h] = past_block(h, slot, j0 + b, ms[h], ls[h])
                s_ref[1 - slot, h] = scores(j_next, h)
        return jnp.concatenate(ms, axis=0), jnp.concatenate(ls, axis=0)

    n_groups = lax.shift_right_logical(i, GROUP_SHIFT)
    m_all, l_all = lax.fori_loop(
        0, n_groups, lambda u, carry: block_group(u * GROUP, GROUP, *carry),
        (jnp.concatenate(ms, axis=0), jnp.concatenate(ls, axis=0)))
    m_ref[...] = m_all
    l_ref[...] = l_all

    @pl.when((i & 2) == 2)
    def _():
        m_all, l_all = block_group(n_groups * GROUP, 2, m_ref[...], l_ref[...])
        m_ref[...] = m_all
        l_ref[...] = l_all

    @pl.when((i & 1) == 1)
    def _():
        for h in range(heads):
            m, l = past_block(h, 0, i - 1, m_ref[h:h + 1, :], l_ref[h:h + 1, :])
            m_ref[h:h + 1, :] = m
            l_ref[h:h + 1, :] = l

    for h in range(heads):
        out_t = acc_ref[h] * (1.0 / l_ref[h:h + 1, :])
        o_ref[:, hs(h)] = out_t.T.astype(BF16)


def _attn_call(qt, k, vt, km, batch, seq):
    n_blocks = seq // MOBA_BLOCK
    return pl.pallas_call(
        functools.partial(_attn_kernel, n_blocks=n_blocks),
        grid=(batch, n_blocks),
        in_specs=[
            pl.BlockSpec((1, D_MODEL, MOBA_BLOCK), lambda b, i: (b * n_blocks + i, 0, 0)),
            pl.BlockSpec((seq, D_MODEL), lambda b, i: (b, 0)),
            pl.BlockSpec((n_blocks, D_MODEL, MOBA_BLOCK), lambda b, i: (b, 0, 0)),
            pl.BlockSpec((n_blocks, D_MODEL), lambda b, i: (b, 0)),
        ],
        out_specs=pl.BlockSpec((MOBA_BLOCK, D_MODEL), lambda b, i: (b * n_blocks + i, 0)),
        out_shape=jax.ShapeDtypeStruct((batch * seq, D_MODEL), BF16),
        scratch_shapes=[
            pltpu.VMEM((2, N_HEADS, MOBA_BLOCK, MOBA_BLOCK), F32),
            pltpu.VMEM((N_HEADS, HEAD_DIM, MOBA_BLOCK), F32),
            pltpu.VMEM((N_HEADS, n_blocks, MOBA_BLOCK), F32),
            pltpu.VMEM((N_HEADS, MOBA_BLOCK), F32),
            pltpu.VMEM((N_HEADS, MOBA_BLOCK), F32),
        ],
        compiler_params=pltpu.CompilerParams(
            dimension_semantics=("arbitrary", "arbitrary"), vmem_limit_bytes=VMEM_LIMIT),
        name="moba_attention",
    )(qt, k, vt, km)


def _attn_mlp_kernel(x_ref, a_ref, g_ref, *rest, final):
    if final:
        gf_ref, wo_ref, wup_ref, wdn_ref, o_ref = rest
    else:
        wo_ref, wup_ref, wdn_ref, o_ref = rest
    x1 = x_ref[...] + jnp.dot(a_ref[...], wo_ref[...], preferred_element_type=F32)
    out = _mlp_residual(x1, g_ref, wup_ref, wdn_ref)
    if final:
        out = _rms(out, gf_ref[...])
    o_ref[...] = out


def _attn_mlp_call(x, a, wo, g, wup, wdn, g_final=None):
    t = x.shape[0]
    tm = ROW_TILE
    final = g_final is not None
    in_specs = [
        pl.BlockSpec((tm, D_MODEL), lambda i: (i, 0)),
        pl.BlockSpec((tm, D_MODEL), lambda i: (i, 0)),
        _resident(*g),
    ]
    args = [x, a, g[0]]
    if final:
        in_specs.append(_resident(*g_final))
        args.append(g_final[0])
    weights = (wo, wup, wdn)
    return pl.pallas_call(
        _with_weights(functools.partial(_attn_mlp_kernel, final=final), n_in=len(args),
                      layers=[w[1] for w in weights], n_out=1),
        grid=(t // tm,),
        in_specs=in_specs + [ANY_SPEC] * len(weights),
        out_specs=pl.BlockSpec((tm, D_MODEL), lambda i: (i, 0)),
        out_shape=jax.ShapeDtypeStruct((t, D_MODEL), F32),
        scratch_shapes=_weight_scratch(weights),
        compiler_params=pltpu.CompilerParams(
            dimension_semantics=("arbitrary",), vmem_limit_bytes=VMEM_LIMIT),
        name="attn_out_mlp",
    )(*args, *[w[0] for w in weights])


def _halo_spec(tm):
    per = tm // HALO
    return pl.BlockSpec((HALO, D_MODEL), lambda i: (jnp.maximum(i * per - 1, 0), 0))


def _pool_mlp_kernel(x_ref, halo_ref, gmix_ref, pw_ref, ps_ref, g_ref, wup_ref, wdn_ref,
                     o_ref, ext_ref, *, tiles_per_seq):
    tm = x_ref.shape[0]
    tile_in_seq = pl.program_id(0) % tiles_per_seq
    x = x_ref[...]
    gmix = gmix_ref[...]
    xn = _rms(x, gmix)
    ext_ref[0:HALO, :] = jnp.where(tile_in_seq == 0, 0.0, _rms(halo_ref[...], gmix))
    ext_ref[HALO:HALO + tm, :] = xn
    pos = tile_in_seq * tm + lax.broadcasted_iota(jnp.int32, (tm, 1), 0)
    ys = []
    for g, w in enumerate(POOL_WINDOWS):
        cs = slice(g * POOL_GROUP_DIM, (g + 1) * POOL_GROUP_DIM)
        own = xn[:, cs]
        total = own
        for back in range(1, w):
            total = total + ext_ref[HALO - back:HALO - back + tm, cs]
        count = jnp.minimum(pos + 1, w).astype(F32)
        pooled = total / count - own
        ys.append(jnp.dot(pooled.astype(BF16), pw_ref[g], preferred_element_type=F32))
    x1 = x + jnp.concatenate(ys, axis=1) * ps_ref[...]
    o_ref[...] = _mlp_residual(x1, g_ref, wup_ref, wdn_ref)


def _pool_mlp_call(x, gmix, pw, ps, g, wup, wdn, seq):
    t = x.shape[0]
    tm = ROW_TILE
    weights = (wup, wdn)
    return pl.pallas_call(
        _with_weights(functools.partial(_pool_mlp_kernel, tiles_per_seq=seq // tm), n_in=6,
                      layers=[w[1] for w in weights], n_out=1),
        grid=(t // tm,),
        in_specs=[
            pl.BlockSpec((tm, D_MODEL), lambda i: (i, 0)),
            _halo_spec(tm),
            _resident(*gmix),
            _resident(*pw),
            _resident(*ps),
            _resident(*g),
        ] + [ANY_SPEC] * len(weights),
        out_specs=pl.BlockSpec((tm, D_MODEL), lambda i: (i, 0)),
        out_shape=jax.ShapeDtypeStruct((t, D_MODEL), F32),
        scratch_shapes=_weight_scratch(weights) + [pltpu.VMEM((HALO + tm, D_MODEL), F32)],
        compiler_params=pltpu.CompilerParams(
            dimension_semantics=("arbitrary",), vmem_limit_bytes=VMEM_LIMIT),
        name="pool_mlp",
    )(x, x, gmix[0], pw[0], ps[0], g[0], *[w[0] for w in weights])


def _conv_mlp_kernel(x_ref, halo_ref, gmix_ref, cw_ref, g_ref, win_ref, wout_ref, wup_ref,
                     wdn_ref, o_ref, xe_ref, z_ref, *, tiles_per_seq):
    tm = x_ref.shape[0]
    tile_in_seq = pl.program_id(0) % tiles_per_seq
    x = x_ref[...]
    gmix = gmix_ref[...]
    xe_ref[0:HALO, :] = _rms(halo_ref[...], gmix).astype(BF16)
    xe_ref[HALO:HALO + tm, :] = _rms(x, gmix).astype(BF16)
    gate_c = jnp.dot(xe_ref[...], win_ref[:, D_MODEL:2 * D_MODEL], preferred_element_type=F32)
    hid = jnp.dot(xe_ref[...], win_ref[:, 2 * D_MODEL:3 * D_MODEL], preferred_element_type=F32)
    z_ref[...] = gate_c * hid

    @pl.when(tile_in_seq == 0)
    def _():
        z_ref[0:HALO, :] = jnp.zeros((HALO, D_MODEL), F32)

    cw = cw_ref[...]
    zc = cw[CONV_WIDTH - 1:CONV_WIDTH, :] * z_ref[HALO:HALO + tm, :]
    for back in range(1, CONV_WIDTH):
        tap = cw[CONV_WIDTH - 1 - back:CONV_WIDTH - back, :]
        zc = zc + tap * z_ref[HALO - back:HALO - back + tm, :]
    gate_b = jnp.dot(xe_ref[HALO:HALO + tm, :], win_ref[:, 0:D_MODEL],
                     preferred_element_type=F32)
    y = jnp.dot((gate_b * zc).astype(BF16), wout_ref[...], preferred_element_type=F32)
    o_ref[...] = _mlp_residual(x + y, g_ref, wup_ref, wdn_ref)


def _conv_mlp_call(x, gmix, win, cw, wout, g, wup, wdn, seq):
    t = x.shape[0]
    tm = ROW_TILE
    weights = (win, wout, wup, wdn)
    return pl.pallas_call(
        _with_weights(functools.partial(_conv_mlp_kernel, tiles_per_seq=seq // tm), n_in=5,
                      layers=[w[1] for w in weights], n_out=1),
        grid=(t // tm,),
        in_specs=[
            pl.BlockSpec((tm, D_MODEL), lambda i: (i, 0)),
            _halo_spec(tm),
            _resident(*gmix),
            _resident(*cw),
            _resident(*g),
        ] + [ANY_SPEC] * len(weights),
        out_specs=pl.BlockSpec((tm, D_MODEL), lambda i: (i, 0)),
        out_shape=jax.ShapeDtypeStruct((t, D_MODEL), F32),
        scratch_shapes=_weight_scratch(weights) + [
            pltpu.VMEM((HALO + tm, D_MODEL), BF16),
            pltpu.VMEM((HALO + tm, D_MODEL), F32),
        ],
        compiler_params=pltpu.CompilerParams(
            dimension_semantics=("arbitrary",), vmem_limit_bytes=VMEM_LIMIT),
        name="conv_mlp",
    )(x, x, gmix[0], cw[0], g[0], *[w[0] for w in weights])


ROPE_PACK = HEAD_DIM // ROT_DIM


def _rope_table_kernel(pos_ref, freq_ref, sign_ref, cos_ref, sin_ref):
    ang = pos_ref[...].astype(F32) * freq_ref[...]
    c = jnp.cos(ang)
    s = jnp.sin(ang) * sign_ref[...]
    rotary = lax.broadcasted_iota(jnp.int32, c.shape, 1) < ROT_DIM
    for g in range(ROPE_PACK):
        shift = (HEAD_DIM - ROT_DIM * g) % HEAD_DIM
        cg = pltpu.roll(c, shift, 1) if shift else c
        sg = pltpu.roll(s, shift, 1) if shift else s
        cos_ref[g] = jnp.where(rotary, cg, 1.0)
        sin_ref[g] = jnp.where(rotary, sg, 0.0)


def _rope_tables(positions):
    n = positions.size
    rows = n // ROPE_PACK
    half = ROT_DIM // 2
    inv_freq = ROPE_THETA ** (-jnp.arange(0, ROT_DIM, 2, dtype=F32) / ROT_DIM)
    freq = jnp.tile(inv_freq, HEAD_DIM // half).reshape(1, HEAD_DIM)
    sign = jnp.tile(jnp.concatenate([-jnp.ones((half,), F32), jnp.ones((half,), F32)]),
                    ROPE_PACK).reshape(1, HEAD_DIM)
    packed = jnp.repeat(positions.reshape(ROPE_PACK, rows).T, ROT_DIM, axis=1)
    rt = ROW_TILE
    table = jax.ShapeDtypeStruct((ROPE_PACK, rows, HEAD_DIM), F32)
    row_spec = pl.BlockSpec((1, HEAD_DIM), lambda i: (0, 0))
    cos, sin = pl.pallas_call(
        _rope_table_kernel,
        grid=(rows // rt,),
        in_specs=[pl.BlockSpec((rt, HEAD_DIM), lambda i: (i, 0)), row_spec, row_spec],
        out_specs=[pl.BlockSpec((ROPE_PACK, rt, HEAD_DIM), lambda i: (0, i, 0))] * 2,
        out_shape=[table, table],
        compiler_params=pltpu.CompilerParams(dimension_semantics=("arbitrary",)),
        name="rope_tables",
    )(packed, freq, sign)
    return cos.reshape(n, HEAD_DIM), sin.reshape(n, HEAD_DIM)


def kernel(x, positions, norm_mix, norm_mlp, attn_w_qkv, attn_w_o, pool_w, pool_scale,
           conv_w_in, conv_w, conv_w_out, mlp_w_up, mlp_w_down, norm_final):
    batch, seq, d = x.shape
    depth = norm_mix.shape[0]
    assert d == D_MODEL and seq % ROW_TILE == 0 and ROW_TILE % MOBA_BLOCK == 0
    assert (depth - 1) % 3 == 0, "the final norm is fused into a MoBA layer's MLP stage"
    cos, sin = _rope_tables(positions)
    h = x.reshape(batch * seq, d)
    rows = lambda v: v.reshape(v.shape[0], 1, v.shape[-1])
    g_mix, g_mlp, p_scale = rows(norm_mix), rows(norm_mlp), rows(pool_scale)
    g_final = norm_final.reshape(1, 1, d)
    w_pool = pool_w.astype(BF16)
    i_attn = i_pool = i_conv = 0
    for i in range(depth):
        kind = i % 3
        mlp = ((g_mlp, i), (mlp_w_up, i), (mlp_w_down, i))
        if kind == 0:
            qt, k, vt, km = _qkv_call(h, (g_mix, i), (attn_w_qkv, i_attn), cos, sin)
            a = _attn_call(qt, k, vt, km.reshape(-1, d), batch, seq)
            h = _attn_mlp_call(h, a, (attn_w_o, i_attn), *mlp,
                               g_final=(g_final, 0) if i == depth - 1 else None)
            i_attn += 1
        elif kind == 1:
            h = _pool_mlp_call(h, (g_mix, i), (w_pool, i_pool), (p_scale, i_pool), *mlp, seq)
            i_pool += 1
        else:
            h = _conv_mlp_call(h, (g_mix, i), (conv_w_in, i_conv), (conv_w, i_conv),
                               (conv_w_out, i_conv), *mlp, seq)
            i_conv += 1
    return h.reshape(batch, seq, d)
```

```python
import functools
import math

import jax
import jax.numpy as jnp
from jax import lax
from jax.experimental import pallas as pl
from jax.experimental.pallas import tpu as pltpu

D_MODEL = 1024
N_HEADS = 8
HEAD_DIM = D_MODEL // N_HEADS
ROT_DIM = HEAD_DIM // 4
ROPE_THETA = 500000.0
MOBA_BLOCK = 256
MOBA_TOPK = 3
POOL_WINDOWS = (2, 4, 8, 16)
POOL_GROUP_DIM = D_MODEL // len(POOL_WINDOWS)
CONV_WIDTH = 3
D_FF = 4 * D_MODEL
NORM_EPS = 1e-6
NEG_INF = -1e30

HALO = 16
POOL_ROWS = 128
ROW_SLABS = 2
ROW_TILE = 512
FF_CHUNK = 1024
ONES_ROWS = 16
GROUP_SHIFT = 2
GROUP = 1 << GROUP_SHIFT
STAGE_BYTES = 512 * 1024
IN_FLIGHT_BYTES = 4 * 1024 * 1024
ANY_SPEC = pl.BlockSpec(memory_space=pl.ANY)
VMEM_LIMIT = 56 * 1024 * 1024
Q_SCALE = (1.0 / math.sqrt(HEAD_DIM)) * math.log2(math.e)

F32 = jnp.float32
BF16 = jnp.bfloat16


def _resident(stacked, layer):
    tail = stacked.shape[1:]
    index = (layer,) + (0,) * len(tail)
    return pl.BlockSpec((None,) + tail, lambda *_: index, pipeline_mode=pl.Buffered(1))


def _chunk_rows(width):
    rows = STAGE_BYTES // (4 * width)
    return rows - rows % 16


def _stage_slots(n_weights):
    return -(-IN_FLIGHT_BYTES // (n_weights * STAGE_BYTES)) + 1


def _weight_scratch(weights):
    slots = _stage_slots(len(weights))
    out = []
    for w in weights:
        rows, width = w[0].shape[1:]
        out += [pltpu.VMEM((rows, width), BF16),
                pltpu.VMEM((slots, _chunk_rows(width), width), F32),
                pltpu.SemaphoreType.DMA((slots,))]
    return out


def _fetch_weights(hbm, layers, fetched):
    plans = []
    for w_hbm, layer, (w_vmem, stage, sem) in zip(hbm, layers, fetched):
        rows = w_vmem.shape[0]
        slots, r = stage.shape[:2]
        assert rows % r == 0 and rows // r >= slots

        def chunk(c, w_hbm=w_hbm, layer=layer, stage=stage, sem=sem, slots=slots, r=r):
            slot = c % slots
            return pltpu.make_async_copy(w_hbm.at[layer, pl.ds(c * r, r), :], stage.at[slot],
                                         sem.at[slot])

        plans.append((rows // r, slots, r, chunk, w_vmem, stage))

    for _, slots, _, chunk, _, _ in plans:
        for c in range(slots - 1):
            chunk(c).start()
    trips = max(p[0] for p in plans)

    def step(c, carry):
        for n, slots, r, chunk, w_vmem, stage in plans:
            def advance(n=n, slots=slots, r=r, chunk=chunk, w_vmem=w_vmem, stage=stage):
                @pl.when(c + slots - 1 < n)
                def _():
                    chunk(c + slots - 1).start()

                chunk(c).wait()
                w_vmem[pl.ds(pl.multiple_of(c * r, r), r), :] = stage[c % slots].astype(BF16)

            if n == trips:
                advance()
            else:
                pl.when(c < n)(advance)
        return carry

    lax.fori_loop(0, trips, step, 0)


def _with_weights(body, n_in, layers, n_out):
    n_w = len(layers)

    def kernel(*refs):
        ins = refs[:n_in]
        hbm = refs[n_in:n_in + n_w]
        outs = refs[n_in + n_w:n_in + n_w + n_out]
        scratch = refs[n_in + n_w + n_out:]
        fetched = [scratch[3 * k:3 * k + 3] for k in range(n_w)]

        @pl.when(pl.program_id(0) == 0)
        def _():
            _fetch_weights(hbm, layers, fetched)

        body(*ins, *[f[0] for f in fetched], *outs, *scratch[3 * n_w:])

    return kernel


def _row_slabs(tm):
    rows = tm // ROW_SLABS
    return [slice(s * rows, (s + 1) * rows) for s in range(ROW_SLABS)]


def _rms(x, g):
    ms = jnp.mean(x * x, axis=-1, keepdims=True)
    return x * lax.rsqrt(ms + NORM_EPS) * g


def _mlp_residual(x1, g_ref, wup_ref, wdn_ref):
    xn = _rms(x1, g_ref[...]).astype(BF16)
    acc = jnp.zeros_like(x1)
    for c in range(D_FF // FF_CHUNK):
        cs = slice(c * FF_CHUNK, (c + 1) * FF_CHUNK)
        h = jnp.dot(xn, wup_ref[:, cs], preferred_element_type=F32)
        h = jnp.maximum(h, 0.0)
        h = (h * h).astype(BF16)
        acc = acc + jnp.dot(h, wdn_ref[cs, :], preferred_element_type=F32)
    return x1 + acc


def _qkv_kernel(x_ref, g_ref, cos_ref, sin_ref, w_ref, qt_ref, k_ref, vt_ref, km_ref):
    tm = x_ref.shape[0]
    xn = _rms(x_ref[...], g_ref[...]).astype(BF16)
    cos = cos_ref[...]
    sin = sin_ref[...]
    lane = lax.broadcasted_iota(jnp.int32, (tm, HEAD_DIM), 1)
    first_half = lane < (ROT_DIM // 2)

    def rope(t):
        parts = []
        for h in range(N_HEADS):
            th = t[:, h * HEAD_DIM:(h + 1) * HEAD_DIM]
            partner = jnp.where(first_half,
                                pltpu.roll(th, HEAD_DIM - ROT_DIM // 2, 1),
                                pltpu.roll(th, ROT_DIM // 2, 1))
            parts.append(th * cos + partner * sin)
        return jnp.concatenate(parts, axis=1)

    q = jnp.dot(xn, w_ref[:, 0:D_MODEL], preferred_element_type=F32)
    q = rope(q) * Q_SCALE
    for b in range(tm // MOBA_BLOCK):
        qt_ref[b] = q[b * MOBA_BLOCK:(b + 1) * MOBA_BLOCK, :].T.astype(BF16)

    k = jnp.dot(xn, w_ref[:, D_MODEL:2 * D_MODEL], preferred_element_type=F32)
    k = rope(k)
    k_ref[...] = k.astype(BF16)
    for b in range(tm // MOBA_BLOCK):
        km_ref[b] = jnp.mean(k[b * MOBA_BLOCK:(b + 1) * MOBA_BLOCK, :], axis=0, keepdims=True)

    v = jnp.dot(xn, w_ref[:, 2 * D_MODEL:3 * D_MODEL], preferred_element_type=F32)
    for b in range(tm // MOBA_BLOCK):
        vt_ref[b] = v[b * MOBA_BLOCK:(b + 1) * MOBA_BLOCK, :].T.astype(BF16)


def _qkv_call(x, g, w, cos, sin):
    t = x.shape[0]
    tm = ROW_TILE
    nb = tm // MOBA_BLOCK
    return pl.pallas_call(
        _with_weights(_qkv_kernel, n_in=4, layers=(w[1],), n_out=4),
        grid=(t // tm,),
        in_specs=[
            pl.BlockSpec((tm, D_MODEL), lambda i: (i, 0)),
            _resident(*g),
            pl.BlockSpec((tm, HEAD_DIM), lambda i: (i, 0)),
            pl.BlockSpec((tm, HEAD_DIM), lambda i: (i, 0)),
            ANY_SPEC,
        ],
        out_specs=[
            pl.BlockSpec((nb, D_MODEL, MOBA_BLOCK), lambda i: (i, 0, 0)),
            pl.BlockSpec((tm, D_MODEL), lambda i: (i, 0)),
            pl.BlockSpec((nb, D_MODEL, MOBA_BLOCK), lambda i: (i, 0, 0)),
            pl.BlockSpec((nb, 1, D_MODEL), lambda i: (i, 0, 0)),
        ],
        out_shape=[
            jax.ShapeDtypeStruct((t // MOBA_BLOCK, D_MODEL, MOBA_BLOCK), BF16),
            jax.ShapeDtypeStruct((t, D_MODEL), BF16),
            jax.ShapeDtypeStruct((t // MOBA_BLOCK, D_MODEL, MOBA_BLOCK), BF16),
            jax.ShapeDtypeStruct((t // MOBA_BLOCK, 1, D_MODEL), F32),
        ],
        scratch_shapes=_weight_scratch((w,)),
        compiler_params=pltpu.CompilerParams(
            dimension_semantics=("arbitrary",), vmem_limit_bytes=VMEM_LIMIT),
        name="moba_qkv",
    )(x, g[0], cos, sin, w[0])


def _attn_kernel(qt_ref, k_ref, vt_ref, km_ref, o_ref, s_ref, acc_ref, bias_ref, m_ref, l_ref, *,
                 n_blocks):
    i = pl.program_id(1)
    heads = acc_ref.shape[0]
    blk = MOBA_BLOCK

    def hs(h):
        return slice(h * HEAD_DIM, (h + 1) * HEAD_DIM)

    def scores(j, h):
        rows = pl.ds(pl.multiple_of(j * blk, blk), blk)
        return jnp.dot(k_ref[rows, hs(h)], qt_ref[0, hs(h), :], preferred_element_type=F32)

    gates = [jnp.dot(km_ref[:, hs(h)].astype(BF16), qt_ref[0, hs(h), :],
                     preferred_element_type=F32) for h in range(heads)]
    own = []
    for h in range(heads):
        own.append(scores(i, h))
        s_ref[0, h] = scores(0, h)

    row = lax.broadcasted_iota(jnp.int32, (n_blocks, blk), 0)
    valid = row < i
    for h in range(heads):
        gate = jnp.where(valid, gates[h], NEG_INF)
        bias = jnp.full((n_blocks, blk), NEG_INF, F32)
        for _ in range(MOBA_TOPK):
            top = jnp.max(gate, axis=0, keepdims=True)
            first = jnp.min(jnp.where(gate == top, row, n_blocks), axis=0, keepdims=True)
            pick = row == first
            bias = jnp.where(pick, 0.0, bias)
            gate = jnp.where(pick, -jnp.inf, gate)
        bias_ref[h] = jnp.where(valid, bias, NEG_INF)

    kpos = lax.broadcasted_iota(jnp.int32, (blk, blk), 0)
    qpos = lax.broadcasted_iota(jnp.int32, (blk, blk), 1)
    causal = kpos <= qpos
    ones_rows = jnp.ones((ONES_ROWS, blk), BF16)

    def weighted_values(j, h, p):
        vt_aug = jnp.concatenate([vt_ref[j, hs(h), :], ones_rows], axis=0)
        out = jnp.dot(vt_aug, p.astype(BF16), preferred_element_type=F32)
        return out[:HEAD_DIM], out[HEAD_DIM:HEAD_DIM + 1]

    ms, ls = [], []
    for h in range(heads):
        s = jnp.where(causal, own[h], NEG_INF)
        m = jnp.max(s, axis=0, keepdims=True)
        pv, p_sum = weighted_values(i, h, jnp.exp2(s - m))
        acc_ref[h] = pv
        ls.append(p_sum)
        ms.append(m)

    def past_block(h, slot, j, m_old, l_old):
        s = s_ref[slot, h]
        bias = bias_ref[h, pl.ds(j, 1), :]
        m_new = jnp.maximum(m_old, jnp.max(s, axis=0, keepdims=True) + bias)
        alpha = jnp.exp2(m_old - m_new)
        pv, p_sum = weighted_values(j, h, jnp.exp2(s - (m_new - bias)))
        acc_ref[h] = alpha * acc_ref[h] + pv
        return m_new, alpha * l_old + p_sum

    def block_group(j0, count, m_all, l_all):
        ms = [m_all[h:h + 1, :] for h in range(heads)]
        ls = [l_all[h:h + 1, :] for h in range(heads)]
        for b in range(count):
            slot = b % 2
            j_next = j0 + b + 1 if b + 1 < count else jnp.minimum(j0 + count, i - 1)
            for h in range(heads):
                ms[h], ls[h] = past_block(h, slot, j0 + b, ms[h], ls[h])
                s_ref[1 - slot, h] = scores(j_next, h)
        return jnp.concatenate(ms, axis=0), jnp.concatenate(ls, axis=0)

    n_groups = lax.shift_right_logical(i, GROUP_SHIFT)
    m_all, l_all = lax.fori_loop(
        0, n_groups, lambda u, carry: block_group(u * GROUP, GROUP, *carry),
        (jnp.concatenate(ms, axis=0), jnp.concatenate(ls, axis=0)))
    m_ref[...] = m_all
    l_ref[...] = l_all

    @pl.when((i & 2) == 2)
    def _():
        m_all, l_all = block_group(n_groups * GROUP, 2, m_ref[...], l_ref[...])
        m_ref[...] = m_all
        l_ref[...] = l_all

    @pl.when((i & 1) == 1)
    def _():
        for h in range(heads):
            m, l = past_block(h, 0, i - 1, m_ref[h:h + 1, :], l_ref[h:h + 1, :])
            m_ref[h:h + 1, :] = m
            l_ref[h:h + 1, :] = l

    for h in range(heads):
        out_t = acc_ref[h] * (1.0 / l_ref[h:h + 1, :])
        o_ref[:, hs(h)] = out_t.T.astype(BF16)


def _attn_call(qt, k, vt, km, batch, seq):
    n_blocks = seq // MOBA_BLOCK
    return pl.pallas_call(
        functools.partial(_attn_kernel, n_blocks=n_blocks),
        grid=(batch, n_blocks),
        in_specs=[
            pl.BlockSpec((1, D_MODEL, MOBA_BLOCK), lambda b, i: (b * n_blocks + i, 0, 0)),
            pl.BlockSpec((seq, D_MODEL), lambda b, i: (b, 0)),
            pl.BlockSpec((n_blocks, D_MODEL, MOBA_BLOCK), lambda b, i: (b, 0, 0)),
            pl.BlockSpec((n_blocks, D_MODEL), lambda b, i: (b, 0)),
        ],
        out_specs=pl.BlockSpec((MOBA_BLOCK, D_MODEL), lambda b, i: (b * n_blocks + i, 0)),
        out_shape=jax.ShapeDtypeStruct((batch * seq, D_MODEL), BF16),
        scratch_shapes=[
            pltpu.VMEM((2, N_HEADS, MOBA_BLOCK, MOBA_BLOCK), F32),
            pltpu.VMEM((N_HEADS, HEAD_DIM, MOBA_BLOCK), F32),
            pltpu.VMEM((N_HEADS, n_blocks, MOBA_BLOCK), F32),
            pltpu.VMEM((N_HEADS, MOBA_BLOCK), F32),
            pltpu.VMEM((N_HEADS, MOBA_BLOCK), F32),
        ],
        compiler_params=pltpu.CompilerParams(
            dimension_semantics=("arbitrary", "arbitrary"), vmem_limit_bytes=VMEM_LIMIT),
        name="moba_attention",
    )(qt, k, vt, km)


def _attn_mlp_kernel(x_ref, a_ref, g_ref, *rest, final):
    if final:
        gf_ref, wo_ref, wup_ref, wdn_ref, o_ref = rest
    else:
        wo_ref, wup_ref, wdn_ref, o_ref = rest
    slabs = _row_slabs(x_ref.shape[0])
    x1 = [x_ref[rows, :] + jnp.dot(a_ref[rows, :], wo_ref[...], preferred_element_type=F32)
          for rows in slabs]
    for rows, x1_slab in zip(slabs, x1):
        out = _mlp_residual(x1_slab, g_ref, wup_ref, wdn_ref)
        if final:
            out = _rms(out, gf_ref[...])
        o_ref[rows, :] = out


def _attn_mlp_call(x, a, wo, g, wup, wdn, g_final=None):
    t = x.shape[0]
    tm = ROW_TILE
    final = g_final is not None
    in_specs = [
        pl.BlockSpec((tm, D_MODEL), lambda i: (i, 0)),
        pl.BlockSpec((tm, D_MODEL), lambda i: (i, 0)),
        _resident(*g),
    ]
    args = [x, a, g[0]]
    if final:
        in_specs.append(_resident(*g_final))
        args.append(g_final[0])
    weights = (wo, wup, wdn)
    return pl.pallas_call(
        _with_weights(functools.partial(_attn_mlp_kernel, final=final), n_in=len(args),
                      layers=[w[1] for w in weights], n_out=1),
        grid=(t // tm,),
        in_specs=in_specs + [ANY_SPEC] * len(weights),
        out_specs=pl.BlockSpec((tm, D_MODEL), lambda i: (i, 0)),
        out_shape=jax.ShapeDtypeStruct((t, D_MODEL), F32),
        scratch_shapes=_weight_scratch(weights),
        compiler_params=pltpu.CompilerParams(
            dimension_semantics=("arbitrary",), vmem_limit_bytes=VMEM_LIMIT),
        name="attn_out_mlp",
    )(*args, *[w[0] for w in weights])


def _halo_spec(tm):
    per = tm // HALO
    return pl.BlockSpec((HALO, D_MODEL), lambda i: (jnp.maximum(i * per - 1, 0), 0))


def _pool_mlp_kernel(x_ref, halo_ref, gmix_ref, pw_ref, ps_ref, g_ref, wup_ref, wdn_ref,
                     o_ref, ext_ref, *, tiles_per_seq):
    tm = x_ref.shape[0]
    tile_in_seq = pl.program_id(0) % tiles_per_seq
    x = x_ref[...]
    gmix = gmix_ref[...]
    xn = _rms(x, gmix)
    ext_ref[0:HALO, :] = jnp.where(tile_in_seq == 0, 0.0,
                                   _rms(halo_ref[...], gmix)).astype(BF16)
    ext_ref[HALO:HALO + tm, :] = xn.astype(BF16)
    pos = tile_in_seq * tm + lax.broadcasted_iota(jnp.int32, (tm, 1), 0)
    ext_rows = POOL_ROWS + HALO
    t_idx = lax.broadcasted_iota(jnp.int32, (POOL_ROWS, ext_rows), 0) + HALO
    e_idx = lax.broadcasted_iota(jnp.int32, (POOL_ROWS, ext_rows), 1)
    bands = [jnp.where(e_idx > t_idx - w, jnp.where(e_idx <= t_idx, 1.0, 0.0), 0.0).astype(BF16)
             for w in POOL_WINDOWS]
    slabs = _row_slabs(tm)
    x1 = []
    for rows in slabs:
        ys = []
        for g, w in enumerate(POOL_WINDOWS):
            cs = slice(g * POOL_GROUP_DIM, (g + 1) * POOL_GROUP_DIM)
            total = jnp.concatenate(
                [jnp.dot(bands[g], ext_ref[r0:r0 + ext_rows, cs], preferred_element_type=F32)
                 for r0 in range(rows.start, rows.stop, POOL_ROWS)], axis=0)
            count = jnp.minimum(pos[rows, :] + 1, w).astype(F32)
            pooled = total / count - xn[rows, cs]
            ys.append(jnp.dot(pooled.astype(BF16), pw_ref[g], preferred_element_type=F32))
        x1.append(x[rows, :] + jnp.concatenate(ys, axis=1) * ps_ref[...])
    for rows, x1_slab in zip(slabs, x1):
        o_ref[rows, :] = _mlp_residual(x1_slab, g_ref, wup_ref, wdn_ref)


def _pool_mlp_call(x, gmix, pw, ps, g, wup, wdn, seq):
    t = x.shape[0]
    tm = ROW_TILE
    weights = (wup, wdn)
    return pl.pallas_call(
        _with_weights(functools.partial(_pool_mlp_kernel, tiles_per_seq=seq // tm), n_in=6,
                      layers=[w[1] for w in weights], n_out=1),
        grid=(t // tm,),
        in_specs=[
            pl.BlockSpec((tm, D_MODEL), lambda i: (i, 0)),
            _halo_spec(tm),
            _resident(*gmix),
            _resident(*pw),
            _resident(*ps),
            _resident(*g),
        ] + [ANY_SPEC] * len(weights),
        out_specs=pl.BlockSpec((tm, D_MODEL), lambda i: (i, 0)),
        out_shape=jax.ShapeDtypeStruct((t, D_MODEL), F32),
        scratch_shapes=_weight_scratch(weights) + [pltpu.VMEM((HALO + tm, D_MODEL), BF16)],
        compiler_params=pltpu.CompilerParams(
            dimension_semantics=("arbitrary",), vmem_limit_bytes=VMEM_LIMIT),
        name="pool_mlp",
    )(x, x, gmix[0], pw[0], ps[0], g[0], *[w[0] for w in weights])


def _conv_mlp_kernel(x_ref, halo_ref, gmix_ref, cw_ref, g_ref, win_ref, wout_ref, wup_ref,
                     wdn_ref, o_ref, xe_ref, z_ref, *, tiles_per_seq):
    tm = x_ref.shape[0]
    tile_in_seq = pl.program_id(0) % tiles_per_seq
    x = x_ref[...]
    gmix = gmix_ref[...]
    xe_ref[0:HALO, :] = _rms(halo_ref[...], gmix).astype(BF16)
    xe_ref[HALO:HALO + tm, :] = _rms(x, gmix).astype(BF16)
    gate_c = jnp.dot(xe_ref[...], win_ref[:, D_MODEL:2 * D_MODEL], preferred_element_type=F32)
    hid = jnp.dot(xe_ref[...], win_ref[:, 2 * D_MODEL:3 * D_MODEL], preferred_element_type=F32)
    z = gate_c * hid
    z_ref[0:HALO, :] = jnp.where(tile_in_seq == 0, 0.0, z[0:HALO, :])
    z_ref[HALO:HALO + tm, :] = z[HALO:HALO + tm, :]

    cw = cw_ref[...]
    slabs = _row_slabs(tm)
    x1 = []
    for rows in slabs:
        zc = cw[CONV_WIDTH - 1:CONV_WIDTH, :] * z_ref[HALO + rows.start:HALO + rows.stop, :]
        for back in range(1, CONV_WIDTH):
            tap = cw[CONV_WIDTH - 1 - back:CONV_WIDTH - back, :]
            zc = zc + tap * z_ref[HALO - back + rows.start:HALO - back + rows.stop, :]
        gate_b = jnp.dot(xe_ref[HALO + rows.start:HALO + rows.stop, :], win_ref[:, 0:D_MODEL],
                         preferred_element_type=F32)
        y = jnp.dot((gate_b * zc).astype(BF16), wout_ref[...], preferred_element_type=F32)
        x1.append(x[rows, :] + y)
    for rows, x1_slab in zip(slabs, x1):
        o_ref[rows, :] = _mlp_residual(x1_slab, g_ref, wup_ref, wdn_ref)


def _conv_mlp_call(x, gmix, win, cw, wout, g, wup, wdn, seq):
    t = x.shape[0]
    tm = ROW_TILE
    weights = (win, wout, wup, wdn)
    return pl.pallas_call(
        _with_weights(functools.partial(_conv_mlp_kernel, tiles_per_seq=seq // tm), n_in=5,
                      layers=[w[1] for w in weights], n_out=1),
        grid=(t // tm,),
        in_specs=[
            pl.BlockSpec((tm, D_MODEL), lambda i: (i, 0)),
            _halo_spec(tm),
            _resident(*gmix),
            _resident(*cw),
            _resident(*g),
        ] + [ANY_SPEC] * len(weights),
        out_specs=pl.BlockSpec((tm, D_MODEL), lambda i: (i, 0)),
        out_shape=jax.ShapeDtypeStruct((t, D_MODEL), F32),
        scratch_shapes=_weight_scratch(weights) + [
            pltpu.VMEM((HALO + tm, D_MODEL), BF16),
            pltpu.VMEM((HALO + tm, D_MODEL), F32),
        ],
        compiler_params=pltpu.CompilerParams(
            dimension_semantics=("arbitrary",), vmem_limit_bytes=VMEM_LIMIT),
        name="conv_mlp",
    )(x, x, gmix[0], cw[0], g[0], *[w[0] for w in weights])


ROPE_PACK = HEAD_DIM // ROT_DIM


def _rope_table_kernel(pos_ref, freq_ref, sign_ref, cos_ref, sin_ref):
    ang = pos_ref[...].astype(F32) * freq_ref[...]
    c = jnp.cos(ang)
    s = jnp.sin(ang) * sign_ref[...]
    rotary = lax.broadcasted_iota(jnp.int32, c.shape, 1) < ROT_DIM
    for g in range(ROPE_PACK):
        shift = (HEAD_DIM - ROT_DIM * g) % HEAD_DIM
        cg = pltpu.roll(c, shift, 1) if shift else c
        sg = pltpu.roll(s, shift, 1) if shift else s
        cos_ref[g] = jnp.where(rotary, cg, 1.0)
        sin_ref[g] = jnp.where(rotary, sg, 0.0)


def _rope_tables(positions):
    n = positions.size
    rows = n // ROPE_PACK
    half = ROT_DIM // 2
    inv_freq = ROPE_THETA ** (-jnp.arange(0, ROT_DIM, 2, dtype=F32) / ROT_DIM)
    freq = jnp.tile(inv_freq, HEAD_DIM // half).reshape(1, HEAD_DIM)
    sign = jnp.tile(jnp.concatenate([-jnp.ones((half,), F32), jnp.ones((half,), F32)]),
                    ROPE_PACK).reshape(1, HEAD_DIM)
    packed = jnp.repeat(positions.reshape(ROPE_PACK, rows).T, ROT_DIM, axis=1)
    rt = ROW_TILE
    table = jax.ShapeDtypeStruct((ROPE_PACK, rows, HEAD_DIM), F32)
    row_spec = pl.BlockSpec((1, HEAD_DIM), lambda i: (0, 0))
    cos, sin = pl.pallas_call(
        _rope_table_kernel,
        grid=(rows // rt,),
        in_specs=[pl.BlockSpec((rt, HEAD_DIM), lambda i: (i, 0)), row_spec, row_spec],
        out_specs=[pl.BlockSpec((ROPE_PACK, rt, HEAD_DIM), lambda i: (0, i, 0))] * 2,
        out_shape=[table, table],
        compiler_params=pltpu.CompilerParams(dimension_semantics=("arbitrary",)),
        name="rope_tables",
    )(packed, freq, sign)
    return cos.reshape(n, HEAD_DIM), sin.reshape(n, HEAD_DIM)


def kernel(x, positions, norm_mix, norm_mlp, attn_w_qkv, attn_w_o, pool_w, pool_scale,
           conv_w_in, conv_w, conv_w_out, mlp_w_up, mlp_w_down, norm_final):
    batch, seq, d = x.shape
    depth = norm_mix.shape[0]
    assert d == D_MODEL and seq % ROW_TILE == 0 and ROW_TILE % MOBA_BLOCK == 0
    assert (depth - 1) % 3 == 0, "the final norm is fused into a MoBA layer's MLP stage"
    cos, sin = _rope_tables(positions)
    h = x.reshape(batch * seq, d)
    rows = lambda v: v.reshape(v.shape[0], 1, v.shape[-1])
    g_mix, g_mlp, p_scale = rows(norm_mix), rows(norm_mlp), rows(pool_scale)
    g_final = norm_final.reshape(1, 1, d)
    w_pool = pool_w.astype(BF16)
    i_attn = i_pool = i_conv = 0
    for i in range(depth):
        kind = i % 3
        mlp = ((g_mlp, i), (mlp_w_up, i), (mlp_w_down, i))
        if kind == 0:
            qt, k, vt, km = _qkv_call(h, (g_mix, i), (attn_w_qkv, i_attn), cos, sin)
            a = _attn_call(qt, k, vt, km.reshape(-1, d), batch, seq)
            h = _attn_mlp_call(h, a, (attn_w_o, i_attn), *mlp,
                               g_final=(g_final, 0) if i == depth - 1 else None)
            i_attn += 1
        elif kind == 1:
            h = _pool_mlp_call(h, (g_mix, i), (w_pool, i_pool), (p_scale, i_pool), *mlp, seq)
            i_pool += 1
        else:
            h = _conv_mlp_call(h, (g_mix, i), (conv_w_in, i_conv), (conv_w, i_conv),
                               (conv_w_out, i_conv), *mlp, seq)
            i_conv += 1
    return h.reshape(batch, seq, d)
```

```python
import functools
import math

import jax
import jax.numpy as jnp
from jax import lax
from jax.experimental import pallas as pl
from jax.experimental.pallas import tpu as pltpu

D_MODEL = 1024
N_HEADS = 8
HEAD_DIM = D_MODEL // N_HEADS
ROT_DIM = HEAD_DIM // 4
ROPE_THETA = 500000.0
MOBA_BLOCK = 256
MOBA_TOPK = 3
POOL_WINDOWS = (2, 4, 8, 16)
POOL_GROUP_DIM = D_MODEL // len(POOL_WINDOWS)
CONV_WIDTH = 3
D_FF = 4 * D_MODEL
NORM_EPS = 1e-6
NEG_INF = -1e30

HALO = 16
POOL_ROWS = 128
ROW_SLABS = 2
ROW_TILE = 512
WIDE_ROW_TILE = 1024
FF_CHUNK = 1024
ONES_ROWS = 16
GROUP_SHIFT = 2
GROUP = 1 << GROUP_SHIFT
STAGE_BYTES = 512 * 1024
IN_FLIGHT_BYTES = 4 * 1024 * 1024
ANY_SPEC = pl.BlockSpec(memory_space=pl.ANY)
VMEM_LIMIT = 56 * 1024 * 1024
Q_SCALE = (1.0 / math.sqrt(HEAD_DIM)) * math.log2(math.e)

F32 = jnp.float32
BF16 = jnp.bfloat16


def _resident(stacked, layer):
    tail = stacked.shape[1:]
    index = (layer,) + (0,) * len(tail)
    return pl.BlockSpec((None,) + tail, lambda *_: index, pipeline_mode=pl.Buffered(1))


def _chunk_rows(width):
    rows = STAGE_BYTES // (4 * width)
    return rows - rows % 16


def _stage_slots(n_weights):
    return -(-IN_FLIGHT_BYTES // (n_weights * STAGE_BYTES)) + 1


def _weight_scratch(weights):
    slots = _stage_slots(len(weights))
    out = []
    for w in weights:
        rows, width = w[0].shape[1:]
        out += [pltpu.VMEM((rows, width), BF16),
                pltpu.VMEM((slots, _chunk_rows(width), width), F32),
                pltpu.SemaphoreType.DMA((slots,))]
    return out


def _fetch_weights(hbm, layers, fetched):
    plans = []
    for w_hbm, layer, (w_vmem, stage, sem) in zip(hbm, layers, fetched):
        rows = w_vmem.shape[0]
        slots, r = stage.shape[:2]
        assert rows % r == 0 and rows // r >= slots

        def chunk(c, w_hbm=w_hbm, layer=layer, stage=stage, sem=sem, slots=slots, r=r):
            slot = c % slots
            return pltpu.make_async_copy(w_hbm.at[layer, pl.ds(c * r, r), :], stage.at[slot],
                                         sem.at[slot])

        plans.append((rows // r, slots, r, chunk, w_vmem, stage))

    for _, slots, _, chunk, _, _ in plans:
        for c in range(slots - 1):
            chunk(c).start()
    trips = max(p[0] for p in plans)

    def step(c, carry):
        for n, slots, r, chunk, w_vmem, stage in plans:
            def advance(n=n, slots=slots, r=r, chunk=chunk, w_vmem=w_vmem, stage=stage):
                @pl.when(c + slots - 1 < n)
                def _():
                    chunk(c + slots - 1).start()

                chunk(c).wait()
                w_vmem[pl.ds(pl.multiple_of(c * r, r), r), :] = stage[c % slots].astype(BF16)

            if n == trips:
                advance()
            else:
                pl.when(c < n)(advance)
        return carry

    lax.fori_loop(0, trips, step, 0)


def _with_weights(body, n_in, layers, n_out):
    n_w = len(layers)

    def kernel(*refs):
        ins = refs[:n_in]
        hbm = refs[n_in:n_in + n_w]
        outs = refs[n_in + n_w:n_in + n_w + n_out]
        scratch = refs[n_in + n_w + n_out:]
        fetched = [scratch[3 * k:3 * k + 3] for k in range(n_w)]

        @pl.when(pl.program_id(0) == 0)
        def _():
            _fetch_weights(hbm, layers, fetched)

        body(*ins, *[f[0] for f in fetched], *outs, *scratch[3 * n_w:])

    return kernel


def _row_slabs(tm):
    rows = tm // ROW_SLABS
    return [slice(s * rows, (s + 1) * rows) for s in range(ROW_SLABS)]


def _rms(x, g):
    ms = jnp.mean(x * x, axis=-1, keepdims=True)
    return x * lax.rsqrt(ms + NORM_EPS) * g


def _mlp_residual(x1, g_ref, wup_ref, wdn_ref):
    xn = _rms(x1, g_ref[...]).astype(BF16)
    acc = jnp.zeros_like(x1)
    for c in range(D_FF // FF_CHUNK):
        cs = slice(c * FF_CHUNK, (c + 1) * FF_CHUNK)
        h = jnp.dot(xn, wup_ref[:, cs], preferred_element_type=F32)
        h = jnp.maximum(h, 0.0)
        h = (h * h).astype(BF16)
        acc = acc + jnp.dot(h, wdn_ref[cs, :], preferred_element_type=F32)
    return x1 + acc


def _qkv_kernel(x_ref, g_ref, cos_ref, sin_ref, w_ref, qt_ref, k_ref, vt_ref, km_ref):
    tm = x_ref.shape[0]
    xn = _rms(x_ref[...], g_ref[...]).astype(BF16)
    cos = cos_ref[...]
    sin = sin_ref[...]
    lane = lax.broadcasted_iota(jnp.int32, (tm, HEAD_DIM), 1)
    first_half = lane < (ROT_DIM // 2)

    def rope(t):
        parts = []
        for h in range(N_HEADS):
            th = t[:, h * HEAD_DIM:(h + 1) * HEAD_DIM]
            partner = jnp.where(first_half,
                                pltpu.roll(th, HEAD_DIM - ROT_DIM // 2, 1),
                                pltpu.roll(th, ROT_DIM // 2, 1))
            parts.append(th * cos + partner * sin)
        return jnp.concatenate(parts, axis=1)

    q = jnp.dot(xn, w_ref[:, 0:D_MODEL], preferred_element_type=F32)
    q = rope(q) * Q_SCALE
    for b in range(tm // MOBA_BLOCK):
        qt_ref[b] = q[b * MOBA_BLOCK:(b + 1) * MOBA_BLOCK, :].T.astype(BF16)

    k = jnp.dot(xn, w_ref[:, D_MODEL:2 * D_MODEL], preferred_element_type=F32)
    k = rope(k)
    k_ref[...] = k.astype(BF16)
    for b in range(tm // MOBA_BLOCK):
        km_ref[b] = jnp.mean(k[b * MOBA_BLOCK:(b + 1) * MOBA_BLOCK, :], axis=0, keepdims=True)

    v = jnp.dot(xn, w_ref[:, 2 * D_MODEL:3 * D_MODEL], preferred_element_type=F32)
    for b in range(tm // MOBA_BLOCK):
        vt_ref[b] = v[b * MOBA_BLOCK:(b + 1) * MOBA_BLOCK, :].T.astype(BF16)


def _qkv_call(x, g, w, cos, sin):
    t = x.shape[0]
    tm = WIDE_ROW_TILE
    nb = tm // MOBA_BLOCK
    return pl.pallas_call(
        _with_weights(_qkv_kernel, n_in=4, layers=(w[1],), n_out=4),
        grid=(t // tm,),
        in_specs=[
            pl.BlockSpec((tm, D_MODEL), lambda i: (i, 0)),
            _resident(*g),
            pl.BlockSpec((tm, HEAD_DIM), lambda i: (i, 0)),
            pl.BlockSpec((tm, HEAD_DIM), lambda i: (i, 0)),
            ANY_SPEC,
        ],
        out_specs=[
            pl.BlockSpec((nb, D_MODEL, MOBA_BLOCK), lambda i: (i, 0, 0)),
            pl.BlockSpec((tm, D_MODEL), lambda i: (i, 0)),
            pl.BlockSpec((nb, D_MODEL, MOBA_BLOCK), lambda i: (i, 0, 0)),
            pl.BlockSpec((nb, 1, D_MODEL), lambda i: (i, 0, 0)),
        ],
        out_shape=[
            jax.ShapeDtypeStruct((t // MOBA_BLOCK, D_MODEL, MOBA_BLOCK), BF16),
            jax.ShapeDtypeStruct((t, D_MODEL), BF16),
            jax.ShapeDtypeStruct((t // MOBA_BLOCK, D_MODEL, MOBA_BLOCK), BF16),
            jax.ShapeDtypeStruct((t // MOBA_BLOCK, 1, D_MODEL), F32),
        ],
        scratch_shapes=_weight_scratch((w,)),
        compiler_params=pltpu.CompilerParams(
            dimension_semantics=("arbitrary",), vmem_limit_bytes=VMEM_LIMIT),
        name="moba_qkv",
    )(x, g[0], cos, sin, w[0])


def _attn_kernel(qt_ref, k_ref, vt_ref, km_ref, o_ref, s_ref, acc_ref, bias_ref, m_ref, l_ref, *,
                 n_blocks):
    i = pl.program_id(1)
    heads = acc_ref.shape[0]
    blk = MOBA_BLOCK

    def hs(h):
        return slice(h * HEAD_DIM, (h + 1) * HEAD_DIM)

    def scores(j, h):
        rows = pl.ds(pl.multiple_of(j * blk, blk), blk)
        return jnp.dot(k_ref[rows, hs(h)], qt_ref[0, hs(h), :], preferred_element_type=F32)

    gates = [jnp.dot(km_ref[:, hs(h)].astype(BF16), qt_ref[0, hs(h), :],
                     preferred_element_type=F32) for h in range(heads)]
    own = []
    for h in range(heads):
        own.append(scores(i, h))
        s_ref[0, h] = scores(0, h)

    row = lax.broadcasted_iota(jnp.int32, (n_blocks, blk), 0)
    valid = row < i
    for h in range(heads):
        gate = jnp.where(valid, gates[h], NEG_INF)
        bias = jnp.full((n_blocks, blk), NEG_INF, F32)
        for _ in range(MOBA_TOPK):
            top = jnp.max(gate, axis=0, keepdims=True)
            first = jnp.min(jnp.where(gate == top, row, n_blocks), axis=0, keepdims=True)
            pick = row == first
            bias = jnp.where(pick, 0.0, bias)
            gate = jnp.where(pick, -jnp.inf, gate)
        bias_ref[h] = jnp.where(valid, bias, NEG_INF)

    kpos = lax.broadcasted_iota(jnp.int32, (blk, blk), 0)
    qpos = lax.broadcasted_iota(jnp.int32, (blk, blk), 1)
    causal = kpos <= qpos
    ones_rows = jnp.ones((ONES_ROWS, blk), BF16)

    def weighted_values(j, h, p):
        vt_aug = jnp.concatenate([vt_ref[j, hs(h), :], ones_rows], axis=0)
        out = jnp.dot(vt_aug, p.astype(BF16), preferred_element_type=F32)
        return out[:HEAD_DIM], out[HEAD_DIM:HEAD_DIM + 1]

    ms, ls = [], []
    for h in range(heads):
        s = jnp.where(causal, own[h], NEG_INF)
        m = jnp.max(s, axis=0, keepdims=True)
        pv, p_sum = weighted_values(i, h, jnp.exp2(s - m))
        acc_ref[h] = pv
        ls.append(p_sum)
        ms.append(m)

    def past_block(h, slot, j, m_old, l_old):
        s = s_ref[slot, h]
        bias = bias_ref[h, pl.ds(j, 1), :]
        m_new = jnp.maximum(m_old, jnp.max(s, axis=0, keepdims=True) + bias)
        alpha = jnp.exp2(m_old - m_new)
        pv, p_sum = weighted_values(j, h, jnp.exp2(s - (m_new - bias)))
        acc_ref[h] = alpha * acc_ref[h] + pv
        return m_new, alpha * l_old + p_sum

    def block_group(j0, count, m_all, l_all):
        ms = [m_all[h:h + 1, :] for h in range(heads)]
        ls = [l_all[h:h + 1, :] for h in range(heads)]
        for b in range(count):
            slot = b % 2
            j_next = j0 + b + 1 if b + 1 < count else jnp.minimum(j0 + count, i - 1)
            for h in range(heads):
                ms[h], ls[h] = past_block(h, slot, j0 + b, ms[h], ls[h])
                s_ref[1 - slot, h] = scores(j_next, h)
        return jnp.concatenate(ms, axis=0), jnp.concatenate(ls, axis=0)

    n_groups = lax.shift_right_logical(i, GROUP_SHIFT)
    m_all, l_all = lax.fori_loop(
        0, n_groups, lambda u, carry: block_group(u * GROUP, GROUP, *carry),
        (jnp.concatenate(ms, axis=0), jnp.concatenate(ls, axis=0)))
    m_ref[...] = m_all
    l_ref[...] = l_all

    for shift in range(GROUP_SHIFT - 1, 0, -1):
        count = 1 << shift

        @pl.when((i & count) == count)
        def _(count=count, shift=shift):
            done = lax.shift_left(lax.shift_right_logical(i, shift + 1), shift + 1)
            m_all, l_all = block_group(done, count, m_ref[...], l_ref[...])
            m_ref[...] = m_all
            l_ref[...] = l_all

    @pl.when((i & 1) == 1)
    def _():
        for h in range(heads):
            m, l = past_block(h, 0, i - 1, m_ref[h:h + 1, :], l_ref[h:h + 1, :])
            m_ref[h:h + 1, :] = m
            l_ref[h:h + 1, :] = l

    for h in range(heads):
        o_ref[0, hs(h), :] = (acc_ref[h] * (1.0 / l_ref[h:h + 1, :])).astype(BF16)


def _attn_call(qt, k, vt, km, batch, seq):
    n_blocks = seq // MOBA_BLOCK
    return pl.pallas_call(
        functools.partial(_attn_kernel, n_blocks=n_blocks),
        grid=(batch, n_blocks),
        in_specs=[
            pl.BlockSpec((1, D_MODEL, MOBA_BLOCK), lambda b, i: (b * n_blocks + i, 0, 0)),
            pl.BlockSpec((seq, D_MODEL), lambda b, i: (b, 0)),
            pl.BlockSpec((n_blocks, D_MODEL, MOBA_BLOCK), lambda b, i: (b, 0, 0)),
            pl.BlockSpec((n_blocks, D_MODEL), lambda b, i: (b, 0)),
        ],
        out_specs=pl.BlockSpec((1, D_MODEL, MOBA_BLOCK), lambda b, i: (b * n_blocks + i, 0, 0)),
        out_shape=jax.ShapeDtypeStruct((batch * n_blocks, D_MODEL, MOBA_BLOCK), BF16),
        scratch_shapes=[
            pltpu.VMEM((2, N_HEADS, MOBA_BLOCK, MOBA_BLOCK), F32),
            pltpu.VMEM((N_HEADS, HEAD_DIM, MOBA_BLOCK), F32),
            pltpu.VMEM((N_HEADS, n_blocks, MOBA_BLOCK), F32),
            pltpu.VMEM((N_HEADS, MOBA_BLOCK), F32),
            pltpu.VMEM((N_HEADS, MOBA_BLOCK), F32),
        ],
        compiler_params=pltpu.CompilerParams(
            dimension_semantics=("arbitrary", "arbitrary"), vmem_limit_bytes=VMEM_LIMIT),
        name="moba_attention",
    )(qt, k, vt, km)


def _attn_mlp_kernel(x_ref, a_ref, g_ref, *rest, final):
    if final:
        gf_ref, wo_ref, wup_ref, wdn_ref, o_ref = rest
    else:
        wo_ref, wup_ref, wdn_ref, o_ref = rest
    slabs = _row_slabs(x_ref.shape[0])
    assert len(slabs) == a_ref.shape[0]
    lhs_dim0 = (((0,), (0,)), ((), ()))
    x1 = [x_ref[rows, :] + lax.dot_general(a_ref[s], wo_ref[...], lhs_dim0,
                                           preferred_element_type=F32)
          for s, rows in enumerate(slabs)]
    for rows, x1_slab in zip(slabs, x1):
        out = _mlp_residual(x1_slab, g_ref, wup_ref, wdn_ref)
        if final:
            out = _rms(out, gf_ref[...])
        o_ref[rows, :] = out


def _attn_mlp_call(x, a, wo, g, wup, wdn, g_final=None):
    t = x.shape[0]
    tm = ROW_TILE
    final = g_final is not None
    in_specs = [
        pl.BlockSpec((tm, D_MODEL), lambda i: (i, 0)),
        pl.BlockSpec((tm // MOBA_BLOCK, D_MODEL, MOBA_BLOCK), lambda i: (i, 0, 0)),
        _resident(*g),
    ]
    args = [x, a, g[0]]
    if final:
        in_specs.append(_resident(*g_final))
        args.append(g_final[0])
    weights = (wo, wup, wdn)
    return pl.pallas_call(
        _with_weights(functools.partial(_attn_mlp_kernel, final=final), n_in=len(args),
                      layers=[w[1] for w in weights], n_out=1),
        grid=(t // tm,),
        in_specs=in_specs + [ANY_SPEC] * len(weights),
        out_specs=pl.BlockSpec((tm, D_MODEL), lambda i: (i, 0)),
        out_shape=jax.ShapeDtypeStruct((t, D_MODEL), F32),
        scratch_shapes=_weight_scratch(weights),
        compiler_params=pltpu.CompilerParams(
            dimension_semantics=("arbitrary",), vmem_limit_bytes=VMEM_LIMIT),
        name="attn_out_mlp",
    )(*args, *[w[0] for w in weights])


def _halo_spec(tm):
    per = tm // HALO
    return pl.BlockSpec((HALO, D_MODEL), lambda i: (jnp.maximum(i * per - 1, 0), 0))


def _pool_mlp_kernel(x_ref, halo_ref, gmix_ref, pw_ref, ps_ref, g_ref, wup_ref, wdn_ref,
                     o_ref, ext_ref, *, tiles_per_seq):
    tm = x_ref.shape[0]
    tile_in_seq = pl.program_id(0) % tiles_per_seq
    x = x_ref[...]
    gmix = gmix_ref[...]
    xn = _rms(x, gmix)
    ext_ref[0:HALO, :] = jnp.where(tile_in_seq == 0, 0.0,
                                   _rms(halo_ref[...], gmix)).astype(BF16)
    ext_ref[HALO:HALO + tm, :] = xn.astype(BF16)
    pos = tile_in_seq * tm + lax.broadcasted_iota(jnp.int32, (tm, 1), 0)
    ext_rows = POOL_ROWS + HALO
    t_idx = lax.broadcasted_iota(jnp.int32, (POOL_ROWS, ext_rows), 0) + HALO
    e_idx = lax.broadcasted_iota(jnp.int32, (POOL_ROWS, ext_rows), 1)
    bands = [jnp.where(e_idx > t_idx - w, jnp.where(e_idx <= t_idx, 1.0, 0.0), 0.0).astype(BF16)
             for w in POOL_WINDOWS]
    slabs = _row_slabs(tm)
    x1 = []
    for rows in slabs:
        ys = []
        for g, w in enumerate(POOL_WINDOWS):
            cs = slice(g * POOL_GROUP_DIM, (g + 1) * POOL_GROUP_DIM)
            total = jnp.concatenate(
                [jnp.dot(bands[g], ext_ref[r0:r0 + ext_rows, cs], preferred_element_type=F32)
                 for r0 in range(rows.start, rows.stop, POOL_ROWS)], axis=0)
            count = jnp.minimum(pos[rows, :] + 1, w).astype(F32)
            pooled = total / count - xn[rows, cs]
            ys.append(jnp.dot(pooled.astype(BF16), pw_ref[g], preferred_element_type=F32))
        x1.append(x[rows, :] + jnp.concatenate(ys, axis=1) * ps_ref[...])
    for rows, x1_slab in zip(slabs, x1):
        o_ref[rows, :] = _mlp_residual(x1_slab, g_ref, wup_ref, wdn_ref)


def _pool_mlp_call(x, gmix, pw, ps, g, wup, wdn, seq):
    t = x.shape[0]
    tm = WIDE_ROW_TILE
    weights = (wup, wdn)
    return pl.pallas_call(
        _with_weights(functools.partial(_pool_mlp_kernel, tiles_per_seq=seq // tm), n_in=6,
                      layers=[w[1] for w in weights], n_out=1),
        grid=(t // tm,),
        in_specs=[
            pl.BlockSpec((tm, D_MODEL), lambda i: (i, 0)),
            _halo_spec(tm),
            _resident(*gmix),
            _resident(*pw),
            _resident(*ps),
            _resident(*g),
        ] + [ANY_SPEC] * len(weights),
        out_specs=pl.BlockSpec((tm, D_MODEL), lambda i: (i, 0)),
        out_shape=jax.ShapeDtypeStruct((t, D_MODEL), F32),
        scratch_shapes=_weight_scratch(weights) + [pltpu.VMEM((HALO + tm, D_MODEL), BF16)],
        compiler_params=pltpu.CompilerParams(
            dimension_semantics=("arbitrary",), vmem_limit_bytes=VMEM_LIMIT),
        name="pool_mlp",
    )(x, x, gmix[0], pw[0], ps[0], g[0], *[w[0] for w in weights])


def _conv_mlp_kernel(x_ref, halo_ref, gmix_ref, cw_ref, g_ref, win_ref, wout_ref, wup_ref,
                     wdn_ref, o_ref, xe_ref, z_ref, *, tiles_per_seq):
    tm = x_ref.shape[0]
    tile_in_seq = pl.program_id(0) % tiles_per_seq
    x = x_ref[...]
    gmix = gmix_ref[...]
    xe_ref[0:HALO, :] = _rms(halo_ref[...], gmix).astype(BF16)
    xe_ref[HALO:HALO + tm, :] = _rms(x, gmix).astype(BF16)
    gate_c = jnp.dot(xe_ref[...], win_ref[:, D_MODEL:2 * D_MODEL], preferred_element_type=F32)
    hid = jnp.dot(xe_ref[...], win_ref[:, 2 * D_MODEL:3 * D_MODEL], preferred_element_type=F32)
    z = gate_c * hid
    z_ref[0:HALO, :] = jnp.where(tile_in_seq == 0, 0.0, z[0:HALO, :])
    z_ref[HALO:HALO + tm, :] = z[HALO:HALO + tm, :]

    cw = cw_ref[...]
    slabs = _row_slabs(tm)
    x1 = []
    for rows in slabs:
        zc = cw[CONV_WIDTH - 1:CONV_WIDTH, :] * z_ref[HALO + rows.start:HALO + rows.stop, :]
        for back in range(1, CONV_WIDTH):
            tap = cw[CONV_WIDTH - 1 - back:CONV_WIDTH - back, :]
            zc = zc + tap * z_ref[HALO - back + rows.start:HALO - back + rows.stop, :]
        gate_b = jnp.dot(xe_ref[HALO + rows.start:HALO + rows.stop, :], win_ref[:, 0:D_MODEL],
                         preferred_element_type=F32)
        y = jnp.dot((gate_b * zc).astype(BF16), wout_ref[...], preferred_element_type=F32)
        x1.append(x[rows, :] + y)
    for rows, x1_slab in zip(slabs, x1):
        o_ref[rows, :] = _mlp_residual(x1_slab, g_ref, wup_ref, wdn_ref)


def _conv_mlp_call(x, gmix, win, cw, wout, g, wup, wdn, seq):
    t = x.shape[0]
    tm = ROW_TILE
    weights = (win, wout, wup, wdn)
    return pl.pallas_call(
        _with_weights(functools.partial(_conv_mlp_kernel, tiles_per_seq=seq // tm), n_in=5,
                      layers=[w[1] for w in weights], n_out=1),
        grid=(t // tm,),
        in_specs=[
            pl.BlockSpec((tm, D_MODEL), lambda i: (i, 0)),
            _halo_spec(tm),
            _resident(*gmix),
            _resident(*cw),
            _resident(*g),
        ] + [ANY_SPEC] * len(weights),
        out_specs=pl.BlockSpec((tm, D_MODEL), lambda i: (i, 0)),
        out_shape=jax.ShapeDtypeStruct((t, D_MODEL), F32),
        scratch_shapes=_weight_scratch(weights) + [
            pltpu.VMEM((HALO + tm, D_MODEL), BF16),
            pltpu.VMEM((HALO + tm, D_MODEL), F32),
        ],
        compiler_params=pltpu.CompilerParams(
            dimension_semantics=("arbitrary",), vmem_limit_bytes=VMEM_LIMIT),
        name="conv_mlp",
    )(x, x, gmix[0], cw[0], g[0], *[w[0] for w in weights])


ROPE_PACK = HEAD_DIM // ROT_DIM


def _rope_table_kernel(pos_ref, freq_ref, sign_ref, cos_ref, sin_ref):
    ang = pos_ref[...].astype(F32) * freq_ref[...]
    c = jnp.cos(ang)
    s = jnp.sin(ang) * sign_ref[...]
    rotary = lax.broadcasted_iota(jnp.int32, c.shape, 1) < ROT_DIM
    for g in range(ROPE_PACK):
        shift = (HEAD_DIM - ROT_DIM * g) % HEAD_DIM
        cg = pltpu.roll(c, shift, 1) if shift else c
        sg = pltpu.roll(s, shift, 1) if shift else s
        cos_ref[g] = jnp.where(rotary, cg, 1.0)
        sin_ref[g] = jnp.where(rotary, sg, 0.0)


def _rope_tables(positions):
    n = positions.size
    rows = n // ROPE_PACK
    half = ROT_DIM // 2
    inv_freq = ROPE_THETA ** (-jnp.arange(0, ROT_DIM, 2, dtype=F32) / ROT_DIM)
    freq = jnp.tile(inv_freq, HEAD_DIM // half).reshape(1, HEAD_DIM)
    sign = jnp.tile(jnp.concatenate([-jnp.ones((half,), F32), jnp.ones((half,), F32)]),
                    ROPE_PACK).reshape(1, HEAD_DIM)
    packed = jnp.repeat(positions.reshape(ROPE_PACK, rows).T, ROT_DIM, axis=1)
    rt = ROW_TILE
    table = jax.ShapeDtypeStruct((ROPE_PACK, rows, HEAD_DIM), F32)
    row_spec = pl.BlockSpec((1, HEAD_DIM), lambda i: (0, 0))
    cos, sin = pl.pallas_call(
        _rope_table_kernel,
        grid=(rows // rt,),
        in_specs=[pl.BlockSpec((rt, HEAD_DIM), lambda i: (i, 0)), row_spec, row_spec],
        out_specs=[pl.BlockSpec((ROPE_PACK, rt, HEAD_DIM), lambda i: (0, i, 0))] * 2,
        out_shape=[table, table],
        compiler_params=pltpu.CompilerParams(dimension_semantics=("arbitrary",)),
        name="rope_tables",
    )(packed, freq, sign)
    return cos.reshape(n, HEAD_DIM), sin.reshape(n, HEAD_DIM)


def kernel(x, positions, norm_mix, norm_mlp, attn_w_qkv, attn_w_o, pool_w, pool_scale,
           conv_w_in, conv_w, conv_w_out, mlp_w_up, mlp_w_down, norm_final):
    batch, seq, d = x.shape
    depth = norm_mix.shape[0]
    assert d == D_MODEL and seq % ROW_TILE == 0 and ROW_TILE % MOBA_BLOCK == 0
    assert (depth - 1) % 3 == 0, "the final norm is fused into a MoBA layer's MLP stage"
    cos, sin = _rope_tables(positions)
    h = x.reshape(batch * seq, d)
    rows = lambda v: v.reshape(v.shape[0], 1, v.shape[-1])
    g_mix, g_mlp, p_scale = rows(norm_mix), rows(norm_mlp), rows(pool_scale)
    g_final = norm_final.reshape(1, 1, d)
    w_pool = pool_w.astype(BF16)
    i_attn = i_pool = i_conv = 0
    for i in range(depth):
        kind = i % 3
        mlp = ((g_mlp, i), (mlp_w_up, i), (mlp_w_down, i))
        if kind == 0:
            qt, k, vt, km = _qkv_call(h, (g_mix, i), (attn_w_qkv, i_attn), cos, sin)
            a = _attn_call(qt, k, vt, km.reshape(-1, d), batch, seq)
            h = _attn_mlp_call(h, a, (attn_w_o, i_attn), *mlp,
                               g_final=(g_final, 0) if i == depth - 1 else None)
            i_attn += 1
        elif kind == 1:
            h = _pool_mlp_call(h, (g_mix, i), (w_pool, i_pool), (p_scale, i_pool), *mlp, seq)
            i_pool += 1
        else:
            h = _conv_mlp_call(h, (g_mix, i), (conv_w_in, i_conv), (conv_w, i_conv),
                               (conv_w_out, i_conv), *mlp, seq)
            i_conv += 1
    return h.reshape(batch, seq, d)
```

```python
import functools
import math

import jax
import jax.numpy as jnp
from jax import lax
from jax.experimental import pallas as pl
from jax.experimental.pallas import tpu as pltpu

D_MODEL = 1024
N_HEADS = 8
HEAD_DIM = D_MODEL // N_HEADS
ROT_DIM = HEAD_DIM // 4
ROPE_THETA = 500000.0
MOBA_BLOCK = 256
MOBA_TOPK = 3
POOL_WINDOWS = (2, 4, 8, 16)
POOL_GROUP_DIM = D_MODEL // len(POOL_WINDOWS)
CONV_WIDTH = 3
D_FF = 4 * D_MODEL
NORM_EPS = 1e-6
NEG_INF = -1e30

HALO = 16
POOL_ROWS = 128
ROW_SLABS = 2
ROW_TILE = 512
WIDE_ROW_TILE = 1024
FF_CHUNK = 1024
ONES_ROWS = 16
GROUP_SHIFT = 2
GROUP = 1 << GROUP_SHIFT
STAGE_BYTES = 512 * 1024
IN_FLIGHT_BYTES = 4 * 1024 * 1024
ANY_SPEC = pl.BlockSpec(memory_space=pl.ANY)
VMEM_LIMIT = 56 * 1024 * 1024
Q_SCALE = (1.0 / math.sqrt(HEAD_DIM)) * math.log2(math.e)

F32 = jnp.float32
BF16 = jnp.bfloat16


def _resident(stacked, layer):
    tail = stacked.shape[1:]
    index = (layer,) + (0,) * len(tail)
    return pl.BlockSpec((None,) + tail, lambda *_: index, pipeline_mode=pl.Buffered(1))


def _chunk_rows(width):
    rows = STAGE_BYTES // (4 * width)
    return rows - rows % 16


def _stage_slots(n_weights):
    return -(-IN_FLIGHT_BYTES // (n_weights * STAGE_BYTES)) + 1


def _weight_scratch(weights):
    slots = _stage_slots(len(weights))
    out = []
    for w in weights:
        rows, width = w[0].shape[1:]
        out += [pltpu.VMEM((rows, width), BF16),
                pltpu.VMEM((slots, _chunk_rows(width), width), F32),
                pltpu.SemaphoreType.DMA((slots,))]
    return out


def _fetch_weights(hbm, layers, fetched):
    plans = []
    for w_hbm, layer, (w_vmem, stage, sem) in zip(hbm, layers, fetched):
        rows = w_vmem.shape[0]
        slots, r = stage.shape[:2]
        assert rows % r == 0 and rows // r >= slots

        def chunk(c, w_hbm=w_hbm, layer=layer, stage=stage, sem=sem, slots=slots, r=r):
            slot = c % slots
            return pltpu.make_async_copy(w_hbm.at[layer, pl.ds(c * r, r), :], stage.at[slot],
                                         sem.at[slot])

        plans.append((rows // r, slots, r, chunk, w_vmem, stage))

    for _, slots, _, chunk, _, _ in plans:
        for c in range(slots - 1):
            chunk(c).start()
    trips = max(p[0] for p in plans)

    def step(c, carry):
        for n, slots, r, chunk, w_vmem, stage in plans:
            def advance(n=n, slots=slots, r=r, chunk=chunk, w_vmem=w_vmem, stage=stage):
                @pl.when(c + slots - 1 < n)
                def _():
                    chunk(c + slots - 1).start()

                chunk(c).wait()
                w_vmem[pl.ds(pl.multiple_of(c * r, r), r), :] = stage[c % slots].astype(BF16)

            if n == trips:
                advance()
            else:
                pl.when(c < n)(advance)
        return carry

    lax.fori_loop(0, trips, step, 0)


def _with_weights(body, n_in, layers, n_out):
    n_w = len(layers)

    def kernel(*refs):
        ins = refs[:n_in]
        hbm = refs[n_in:n_in + n_w]
        outs = refs[n_in + n_w:n_in + n_w + n_out]
        scratch = refs[n_in + n_w + n_out:]
        fetched = [scratch[3 * k:3 * k + 3] for k in range(n_w)]

        @pl.when(pl.program_id(0) == 0)
        def _():
            _fetch_weights(hbm, layers, fetched)

        body(*ins, *[f[0] for f in fetched], *outs, *scratch[3 * n_w:])

    return kernel


def _row_slabs(tm):
    rows = tm // ROW_SLABS
    return [slice(s * rows, (s + 1) * rows) for s in range(ROW_SLABS)]


def _rms(x, g):
    ms = jnp.mean(x * x, axis=-1, keepdims=True)
    return x * lax.rsqrt(ms + NORM_EPS) * g


def _mlp_residual(x1, g_ref, wup_ref, wdn_ref):
    xn = _rms(x1, g_ref[...]).astype(BF16)
    acc = jnp.zeros_like(x1)
    for c in range(D_FF // FF_CHUNK):
        cs = slice(c * FF_CHUNK, (c + 1) * FF_CHUNK)
        h = jnp.dot(xn, wup_ref[:, cs], preferred_element_type=F32)
        h = jnp.maximum(h, 0.0)
        h = (h * h).astype(BF16)
        acc = acc + jnp.dot(h, wdn_ref[cs, :], preferred_element_type=F32)
    return x1 + acc


def _qkv_kernel(x_ref, g_ref, cos_ref, sin_ref, w_ref, qt_ref, k_ref, vt_ref, km_ref):
    tm = x_ref.shape[0]
    xn = _rms(x_ref[...], g_ref[...]).astype(BF16)
    cos = cos_ref[...]
    sin = sin_ref[...]
    lane = lax.broadcasted_iota(jnp.int32, (tm, HEAD_DIM), 1)
    first_half = lane < (ROT_DIM // 2)

    def rope(t):
        parts = []
        for h in range(N_HEADS):
            th = t[:, h * HEAD_DIM:(h + 1) * HEAD_DIM]
            partner = jnp.where(first_half,
                                pltpu.roll(th, HEAD_DIM - ROT_DIM // 2, 1),
                                pltpu.roll(th, ROT_DIM // 2, 1))
            parts.append(th * cos + partner * sin)
        return jnp.concatenate(parts, axis=1)

    q = jnp.dot(xn, w_ref[:, 0:D_MODEL], preferred_element_type=F32)
    q = rope(q) * Q_SCALE
    for b in range(tm // MOBA_BLOCK):
        qt_ref[b] = q[b * MOBA_BLOCK:(b + 1) * MOBA_BLOCK, :].T.astype(BF16)

    k = jnp.dot(xn, w_ref[:, D_MODEL:2 * D_MODEL], preferred_element_type=F32)
    k = rope(k)
    k_ref[...] = k.astype(BF16)
    for b in range(tm // MOBA_BLOCK):
        km_ref[b] = jnp.mean(k[b * MOBA_BLOCK:(b + 1) * MOBA_BLOCK, :], axis=0, keepdims=True)

    v = jnp.dot(xn, w_ref[:, 2 * D_MODEL:3 * D_MODEL], preferred_element_type=F32)
    for b in range(tm // MOBA_BLOCK):
        vt_ref[b] = v[b * MOBA_BLOCK:(b + 1) * MOBA_BLOCK, :].T.astype(BF16)


def _qkv_call(x, g, w, cos, sin):
    t = x.shape[0]
    tm = WIDE_ROW_TILE
    nb = tm // MOBA_BLOCK
    return pl.pallas_call(
        _with_weights(_qkv_kernel, n_in=4, layers=(w[1],), n_out=4),
        grid=(t // tm,),
        in_specs=[
            pl.BlockSpec((tm, D_MODEL), lambda i: (i, 0)),
            _resident(*g),
            pl.BlockSpec((tm, HEAD_DIM), lambda i: (i, 0)),
            pl.BlockSpec((tm, HEAD_DIM), lambda i: (i, 0)),
            ANY_SPEC,
        ],
        out_specs=[
            pl.BlockSpec((nb, D_MODEL, MOBA_BLOCK), lambda i: (i, 0, 0)),
            pl.BlockSpec((tm, D_MODEL), lambda i: (i, 0)),
            pl.BlockSpec((nb, D_MODEL, MOBA_BLOCK), lambda i: (i, 0, 0)),
            pl.BlockSpec((nb, 1, D_MODEL), lambda i: (i, 0, 0)),
        ],
        out_shape=[
            jax.ShapeDtypeStruct((t // MOBA_BLOCK, D_MODEL, MOBA_BLOCK), BF16),
            jax.ShapeDtypeStruct((t, D_MODEL), BF16),
            jax.ShapeDtypeStruct((t // MOBA_BLOCK, D_MODEL, MOBA_BLOCK), BF16),
            jax.ShapeDtypeStruct((t // MOBA_BLOCK, 1, D_MODEL), F32),
        ],
        scratch_shapes=_weight_scratch((w,)),
        compiler_params=pltpu.CompilerParams(
            dimension_semantics=("arbitrary",), vmem_limit_bytes=VMEM_LIMIT),
        name="moba_qkv",
    )(x, g[0], cos, sin, w[0])


def _attn_kernel(qt_ref, k_ref, vt_ref, km_ref, o_ref, s_ref, acc_ref, bias_ref, m_ref, l_ref, *,
                 n_blocks):
    i = pl.program_id(1)
    heads = acc_ref.shape[0]
    blk = MOBA_BLOCK

    def hs(h):
        return slice(h * HEAD_DIM, (h + 1) * HEAD_DIM)

    def scores(j, h):
        rows = pl.ds(pl.multiple_of(j * blk, blk), blk)
        return jnp.dot(k_ref[rows, hs(h)], qt_ref[0, hs(h), :], preferred_element_type=F32)

    own = [scores(i, h) for h in range(heads)]
    gates = [jnp.dot(km_ref[:, hs(h)].astype(BF16), qt_ref[0, hs(h), :],
                     preferred_element_type=F32) for h in range(heads)]
    for h in range(heads):
        s_ref[0, h] = scores(0, h)

    row = lax.broadcasted_iota(jnp.int32, (n_blocks, blk), 0)
    valid = row < i
    for h in range(heads):
        gate = jnp.where(valid, gates[h], NEG_INF)
        bias = jnp.full((n_blocks, blk), NEG_INF, F32)
        for _ in range(MOBA_TOPK):
            top = jnp.max(gate, axis=0, keepdims=True)
            first = jnp.min(jnp.where(gate == top, row, n_blocks), axis=0, keepdims=True)
            pick = row == first
            bias = jnp.where(pick, 0.0, bias)
            gate = jnp.where(pick, -jnp.inf, gate)
        bias_ref[h] = jnp.where(valid, bias, NEG_INF)

    kpos = lax.broadcasted_iota(jnp.int32, (blk, blk), 0)
    qpos = lax.broadcasted_iota(jnp.int32, (blk, blk), 1)
    causal = kpos <= qpos
    ones_rows = jnp.ones((ONES_ROWS, blk), BF16)

    def weighted_values(j, h, p):
        vt_aug = jnp.concatenate([vt_ref[j, hs(h), :], ones_rows], axis=0)
        out = jnp.dot(vt_aug, p.astype(BF16), preferred_element_type=F32)
        return out[:HEAD_DIM], out[HEAD_DIM:HEAD_DIM + 1]

    ms, ls = [], []
    for h in range(heads):
        s = jnp.where(causal, own[h], NEG_INF)
        m = jnp.max(s, axis=0, keepdims=True)
        pv, p_sum = weighted_values(i, h, jnp.exp2(s - m))
        acc_ref[h] = pv
        ls.append(p_sum)
        ms.append(m)

    def past_block(h, slot, j, m_old, l_old):
        s = s_ref[slot, h]
        bias = bias_ref[h, pl.ds(j, 1), :]
        m_new = jnp.maximum(m_old, jnp.max(s, axis=0, keepdims=True) + bias)
        alpha = jnp.exp2(m_old - m_new)
        pv, p_sum = weighted_values(j, h, jnp.exp2(s - (m_new - bias)))
        acc_ref[h] = alpha * acc_ref[h] + pv
        return m_new, alpha * l_old + p_sum

    def block_group(j0, count, m_all, l_all):
        ms = [m_all[h:h + 1, :] for h in range(heads)]
        ls = [l_all[h:h + 1, :] for h in range(heads)]
        for b in range(count):
            slot = b % 2
            j_next = j0 + b + 1 if b + 1 < count else jnp.minimum(j0 + count, i - 1)
            for h in range(heads):
                ms[h], ls[h] = past_block(h, slot, j0 + b, ms[h], ls[h])
                s_ref[1 - slot, h] = scores(j_next, h)
        return jnp.concatenate(ms, axis=0), jnp.concatenate(ls, axis=0)

    n_groups = lax.shift_right_logical(i, GROUP_SHIFT)
    m_all, l_all = lax.fori_loop(
        0, n_groups, lambda u, carry: block_group(u * GROUP, GROUP, *carry),
        (jnp.concatenate(ms, axis=0), jnp.concatenate(ls, axis=0)))
    m_ref[...] = m_all
    l_ref[...] = l_all

    for shift in range(GROUP_SHIFT - 1, 0, -1):
        count = 1 << shift

        @pl.when((i & count) == count)
        def _(count=count, shift=shift):
            done = lax.shift_left(lax.shift_right_logical(i, shift + 1), shift + 1)
            m_all, l_all = block_group(done, count, m_ref[...], l_ref[...])
            m_ref[...] = m_all
            l_ref[...] = l_all

    @pl.when((i & 1) == 1)
    def _():
        for h in range(heads):
            m, l = past_block(h, 0, i - 1, m_ref[h:h + 1, :], l_ref[h:h + 1, :])
            m_ref[h:h + 1, :] = m
            l_ref[h:h + 1, :] = l

    for h in range(heads):
        o_ref[0, hs(h), :] = (acc_ref[h] * (1.0 / l_ref[h:h + 1, :])).astype(BF16)


def _attn_call(qt, k, vt, km, batch, seq):
    n_blocks = seq // MOBA_BLOCK
    return pl.pallas_call(
        functools.partial(_attn_kernel, n_blocks=n_blocks),
        grid=(batch, n_blocks),
        in_specs=[
            pl.BlockSpec((1, D_MODEL, MOBA_BLOCK), lambda b, i: (b * n_blocks + i, 0, 0)),
            pl.BlockSpec((seq, D_MODEL), lambda b, i: (b, 0)),
            pl.BlockSpec((n_blocks, D_MODEL, MOBA_BLOCK), lambda b, i: (b, 0, 0)),
            pl.BlockSpec((n_blocks, D_MODEL), lambda b, i: (b, 0)),
        ],
        out_specs=pl.BlockSpec((1, D_MODEL, MOBA_BLOCK), lambda b, i: (b * n_blocks + i, 0, 0)),
        out_shape=jax.ShapeDtypeStruct((batch * n_blocks, D_MODEL, MOBA_BLOCK), BF16),
        scratch_shapes=[
            pltpu.VMEM((2, N_HEADS, MOBA_BLOCK, MOBA_BLOCK), F32),
            pltpu.VMEM((N_HEADS, HEAD_DIM, MOBA_BLOCK), F32),
            pltpu.VMEM((N_HEADS, n_blocks, MOBA_BLOCK), F32),
            pltpu.VMEM((N_HEADS, MOBA_BLOCK), F32),
            pltpu.VMEM((N_HEADS, MOBA_BLOCK), F32),
        ],
        compiler_params=pltpu.CompilerParams(
            dimension_semantics=("arbitrary", "arbitrary"), vmem_limit_bytes=VMEM_LIMIT),
        name="moba_attention",
    )(qt, k, vt, km)


def _attn_mlp_kernel(x_ref, a_ref, g_ref, *rest, final):
    if final:
        gf_ref, wo_ref, wup_ref, wdn_ref, o_ref = rest
    else:
        wo_ref, wup_ref, wdn_ref, o_ref = rest
    slabs = [slice(s * MOBA_BLOCK, (s + 1) * MOBA_BLOCK) for s in range(a_ref.shape[0])]
    lhs_dim0 = (((0,), (0,)), ((), ()))
    x1 = [x_ref[rows, :] + lax.dot_general(a_ref[s], wo_ref[...], lhs_dim0,
                                           preferred_element_type=F32)
          for s, rows in enumerate(slabs)]
    for rows, x1_slab in zip(slabs, x1):
        out = _mlp_residual(x1_slab, g_ref, wup_ref, wdn_ref)
        if final:
            out = _rms(out, gf_ref[...])
        o_ref[rows, :] = out


def _attn_mlp_call(x, a, wo, g, wup, wdn, g_final=None):
    t = x.shape[0]
    tm = WIDE_ROW_TILE
    final = g_final is not None
    in_specs = [
        pl.BlockSpec((tm, D_MODEL), lambda i: (i, 0)),
        pl.BlockSpec((tm // MOBA_BLOCK, D_MODEL, MOBA_BLOCK), lambda i: (i, 0, 0)),
        _resident(*g),
    ]
    args = [x, a, g[0]]
    if final:
        in_specs.append(_resident(*g_final))
        args.append(g_final[0])
    weights = (wo, wup, wdn)
    return pl.pallas_call(
        _with_weights(functools.partial(_attn_mlp_kernel, final=final), n_in=len(args),
                      layers=[w[1] for w in weights], n_out=1),
        grid=(t // tm,),
        in_specs=in_specs + [ANY_SPEC] * len(weights),
        out_specs=pl.BlockSpec((tm, D_MODEL), lambda i: (i, 0)),
        out_shape=jax.ShapeDtypeStruct((t, D_MODEL), F32),
        scratch_shapes=_weight_scratch(weights),
        compiler_params=pltpu.CompilerParams(
            dimension_semantics=("arbitrary",), vmem_limit_bytes=VMEM_LIMIT),
        name="attn_out_mlp",
    )(*args, *[w[0] for w in weights])


def _halo_spec(tm):
    per = tm // HALO
    return pl.BlockSpec((HALO, D_MODEL), lambda i: (jnp.maximum(i * per - 1, 0), 0))


def _pool_mlp_kernel(x_ref, halo_ref, gmix_ref, pw_ref, ps_ref, g_ref, wup_ref, wdn_ref,
                     o_ref, ext_ref, *, tiles_per_seq):
    tm = x_ref.shape[0]
    tile_in_seq = pl.program_id(0) % tiles_per_seq
    x = x_ref[...]
    gmix = gmix_ref[...]
    xn = _rms(x, gmix)
    ext_ref[0:HALO, :] = jnp.where(tile_in_seq == 0, 0.0,
                                   _rms(halo_ref[...], gmix)).astype(BF16)
    ext_ref[HALO:HALO + tm, :] = xn.astype(BF16)
    pos = tile_in_seq * tm + lax.broadcasted_iota(jnp.int32, (tm, 1), 0)
    ext_rows = POOL_ROWS + HALO
    t_idx = lax.broadcasted_iota(jnp.int32, (POOL_ROWS, ext_rows), 0) + HALO
    e_idx = lax.broadcasted_iota(jnp.int32, (POOL_ROWS, ext_rows), 1)
    bands = [jnp.where(e_idx > t_idx - w, jnp.where(e_idx <= t_idx, 1.0, 0.0), 0.0).astype(BF16)
             for w in POOL_WINDOWS]
    slabs = _row_slabs(tm)
    x1 = []
    for rows in slabs:
        ys = []
        for g, w in enumerate(POOL_WINDOWS):
            cs = slice(g * POOL_GROUP_DIM, (g + 1) * POOL_GROUP_DIM)
            total = jnp.concatenate(
                [jnp.dot(bands[g], ext_ref[r0:r0 + ext_rows, cs], preferred_element_type=F32)
                 for r0 in range(rows.start, rows.stop, POOL_ROWS)], axis=0)
            count = jnp.minimum(pos[rows, :] + 1, w).astype(F32)
            pooled = total / count - xn[rows, cs]
            ys.append(jnp.dot(pooled.astype(BF16), pw_ref[g], preferred_element_type=F32))
        x1.append(x[rows, :] + jnp.concatenate(ys, axis=1) * ps_ref[...])
    for rows, x1_slab in zip(slabs, x1):
        o_ref[rows, :] = _mlp_residual(x1_slab, g_ref, wup_ref, wdn_ref)


def _pool_mlp_call(x, gmix, pw, ps, g, wup, wdn, seq):
    t = x.shape[0]
    tm = ROW_TILE
    weights = (wup, wdn)
    return pl.pallas_call(
        _with_weights(functools.partial(_pool_mlp_kernel, tiles_per_seq=seq // tm), n_in=6,
                      layers=[w[1] for w in weights], n_out=1),
        grid=(t // tm,),
        in_specs=[
            pl.BlockSpec((tm, D_MODEL), lambda i: (i, 0)),
            _halo_spec(tm),
            _resident(*gmix),
            _resident(*pw),
            _resident(*ps),
            _resident(*g),
        ] + [ANY_SPEC] * len(weights),
        out_specs=pl.BlockSpec((tm, D_MODEL), lambda i: (i, 0)),
        out_shape=jax.ShapeDtypeStruct((t, D_MODEL), F32),
        scratch_shapes=_weight_scratch(weights) + [pltpu.VMEM((HALO + tm, D_MODEL), BF16)],
        compiler_params=pltpu.CompilerParams(
            dimension_semantics=("arbitrary",), vmem_limit_bytes=VMEM_LIMIT),
        name="pool_mlp",
    )(x, x, gmix[0], pw[0], ps[0], g[0], *[w[0] for w in weights])


def _conv_mlp_kernel(x_ref, halo_ref, gmix_ref, cw_ref, g_ref, win_ref, wout_ref, wup_ref,
                     wdn_ref, o_ref, xe_ref, z_ref, *, tiles_per_seq):
    tm = x_ref.shape[0]
    tile_in_seq = pl.program_id(0) % tiles_per_seq
    x = x_ref[...]
    gmix = gmix_ref[...]
    xe_ref[0:HALO, :] = _rms(halo_ref[...], gmix).astype(BF16)
    xe_ref[HALO:HALO + tm, :] = _rms(x, gmix).astype(BF16)
    gate_c = jnp.dot(xe_ref[...], win_ref[:, D_MODEL:2 * D_MODEL], preferred_element_type=F32)
    hid = jnp.dot(xe_ref[...], win_ref[:, 2 * D_MODEL:3 * D_MODEL], preferred_element_type=F32)
    z = gate_c * hid
    z_ref[0:HALO, :] = jnp.where(tile_in_seq == 0, 0.0, z[0:HALO, :])
    z_ref[HALO:HALO + tm, :] = z[HALO:HALO + tm, :]

    cw = cw_ref[...]
    slabs = _row_slabs(tm)
    x1 = []
    for rows in slabs:
        zc = cw[CONV_WIDTH - 1:CONV_WIDTH, :] * z_ref[HALO + rows.start:HALO + rows.stop, :]
        for back in range(1, CONV_WIDTH):
            tap = cw[CONV_WIDTH - 1 - back:CONV_WIDTH - back, :]
            zc = zc + tap * z_ref[HALO - back + rows.start:HALO - back + rows.stop, :]
        gate_b = jnp.dot(xe_ref[HALO + rows.start:HALO + rows.stop, :], win_ref[:, 0:D_MODEL],
                         preferred_element_type=F32)
        y = jnp.dot((gate_b * zc).astype(BF16), wout_ref[...], preferred_element_type=F32)
        x1.append(x[rows, :] + y)
    for rows, x1_slab in zip(slabs, x1):
        o_ref[rows, :] = _mlp_residual(x1_slab, g_ref, wup_ref, wdn_ref)


def _conv_mlp_call(x, gmix, win, cw, wout, g, wup, wdn, seq):
    t = x.shape[0]
    tm = ROW_TILE
    weights = (win, wout, wup, wdn)
    return pl.pallas_call(
        _with_weights(functools.partial(_conv_mlp_kernel, tiles_per_seq=seq // tm), n_in=5,
                      layers=[w[1] for w in weights], n_out=1),
        grid=(t // tm,),
        in_specs=[
            pl.BlockSpec((tm, D_MODEL), lambda i: (i, 0)),
            _halo_spec(tm),
            _resident(*gmix),
            _resident(*cw),
            _resident(*g),
        ] + [ANY_SPEC] * len(weights),
        out_specs=pl.BlockSpec((tm, D_MODEL), lambda i: (i, 0)),
        out_shape=jax.ShapeDtypeStruct((t, D_MODEL), F32),
        scratch_shapes=_weight_scratch(weights) + [
            pltpu.VMEM((HALO + tm, D_MODEL), BF16),
            pltpu.VMEM((HALO + tm, D_MODEL), F32),
        ],
        compiler_params=pltpu.CompilerParams(
            dimension_semantics=("arbitrary",), vmem_limit_bytes=VMEM_LIMIT),
        name="conv_mlp",
    )(x, x, gmix[0], cw[0], g[0], *[w[0] for w in weights])


ROPE_PACK = HEAD_DIM // ROT_DIM


def _rope_table_kernel(pos_ref, freq_ref, sign_ref, cos_ref, sin_ref):
    ang = pos_ref[...].astype(F32) * freq_ref[...]
    c = jnp.cos(ang)
    s = jnp.sin(ang) * sign_ref[...]
    rotary = lax.broadcasted_iota(jnp.int32, c.shape, 1) < ROT_DIM
    for g in range(ROPE_PACK):
        shift = (HEAD_DIM - ROT_DIM * g) % HEAD_DIM
        cg = pltpu.roll(c, shift, 1) if shift else c
        sg = pltpu.roll(s, shift, 1) if shift else s
        cos_ref[g] = jnp.where(rotary, cg, 1.0)
        sin_ref[g] = jnp.where(rotary, sg, 0.0)


def _rope_tables(positions):
    n = positions.size
    rows = n // ROPE_PACK
    half = ROT_DIM // 2
    inv_freq = ROPE_THETA ** (-jnp.arange(0, ROT_DIM, 2, dtype=F32) / ROT_DIM)
    freq = jnp.tile(inv_freq, HEAD_DIM // half).reshape(1, HEAD_DIM)
    sign = jnp.tile(jnp.concatenate([-jnp.ones((half,), F32), jnp.ones((half,), F32)]),
                    ROPE_PACK).reshape(1, HEAD_DIM)
    packed = jnp.repeat(positions.reshape(ROPE_PACK, rows).T, ROT_DIM, axis=1)
    rt = ROW_TILE
    table = jax.ShapeDtypeStruct((ROPE_PACK, rows, HEAD_DIM), F32)
    row_spec = pl.BlockSpec((1, HEAD_DIM), lambda i: (0, 0))
    cos, sin = pl.pallas_call(
        _rope_table_kernel,
        grid=(rows // rt,),
        in_specs=[pl.BlockSpec((rt, HEAD_DIM), lambda i: (i, 0)), row_spec, row_spec],
        out_specs=[pl.BlockSpec((ROPE_PACK, rt, HEAD_DIM), lambda i: (0, i, 0))] * 2,
        out_shape=[table, table],
        compiler_params=pltpu.CompilerParams(dimension_semantics=("arbitrary",)),
        name="rope_tables",
    )(packed, freq, sign)
    return cos.reshape(n, HEAD_DIM), sin.reshape(n, HEAD_DIM)


def kernel(x, positions, norm_mix, norm_mlp, attn_w_qkv, attn_w_o, pool_w, pool_scale,
           conv_w_in, conv_w, conv_w_out, mlp_w_up, mlp_w_down, norm_final):
    batch, seq, d = x.shape
    depth = norm_mix.shape[0]
    assert d == D_MODEL and seq % ROW_TILE == 0 and ROW_TILE % MOBA_BLOCK == 0
    assert (depth - 1) % 3 == 0, "the final norm is fused into a MoBA layer's MLP stage"
    cos, sin = _rope_tables(positions)
    h = x.reshape(batch * seq, d)
    rows = lambda v: v.reshape(v.shape[0], 1, v.shape[-1])
    g_mix, g_mlp, p_scale = rows(norm_mix), rows(norm_mlp), rows(pool_scale)
    g_final = norm_final.reshape(1, 1, d)
    w_pool = pool_w.astype(BF16)
    i_attn = i_pool = i_conv = 0
    for i in range(depth):
        kind = i % 3
        mlp = ((g_mlp, i), (mlp_w_up, i), (mlp_w_down, i))
        if kind == 0:
            qt, k, vt, km = _qkv_call(h, (g_mix, i), (attn_w_qkv, i_attn), cos, sin)
            a = _attn_call(qt, k, vt, km.reshape(-1, d), batch, seq)
            h = _attn_mlp_call(h, a, (attn_w_o, i_attn), *mlp,
                               g_final=(g_final, 0) if i == depth - 1 else None)
            i_attn += 1
        elif kind == 1:
            h = _pool_mlp_call(h, (g_mix, i), (w_pool, i_pool), (p_scale, i_pool), *mlp, seq)
            i_pool += 1
        else:
            h = _conv_mlp_call(h, (g_mix, i), (conv_w_in, i_conv), (conv_w, i_conv),
                               (conv_w_out, i_conv), *mlp, seq)
            i_conv += 1
    return h.reshape(batch, seq, d)
```

```python
import functools
import math

import jax
import jax.numpy as jnp
from jax import lax
from jax.experimental import pallas as pl
from jax.experimental.pallas import tpu as pltpu

D_MODEL = 1024
N_HEADS = 8
HEAD_DIM = D_MODEL // N_HEADS
ROT_DIM = HEAD_DIM // 4
ROPE_THETA = 500000.0
MOBA_BLOCK = 256
MOBA_TOPK = 3
POOL_WINDOWS = (2, 4, 8, 16)
POOL_GROUP_DIM = D_MODEL // len(POOL_WINDOWS)
CONV_WIDTH = 3
D_FF = 4 * D_MODEL
NORM_EPS = 1e-6
NEG_INF = -1e30

HALO = 16
POOL_ROWS = 128
ROW_SLABS = 2
ROW_TILE = 512
WIDE_ROW_TILE = 1024
FF_CHUNK = 1024
ONES_ROWS = 16
STAGE_BYTES = 512 * 1024
IN_FLIGHT_BYTES = 4 * 1024 * 1024
ANY_SPEC = pl.BlockSpec(memory_space=pl.ANY)
VMEM_LIMIT = 56 * 1024 * 1024
Q_SCALE = (1.0 / math.sqrt(HEAD_DIM)) * math.log2(math.e)

F32 = jnp.float32
BF16 = jnp.bfloat16


def _resident(stacked, layer):
    tail = stacked.shape[1:]
    index = (layer,) + (0,) * len(tail)
    return pl.BlockSpec((None,) + tail, lambda *_: index, pipeline_mode=pl.Buffered(1))


def _chunk_rows(width):
    rows = STAGE_BYTES // (4 * width)
    return rows - rows % 16


def _stage_slots(n_weights):
    return -(-IN_FLIGHT_BYTES // (n_weights * STAGE_BYTES)) + 1


def _weight_scratch(weights):
    slots = _stage_slots(len(weights))
    out = []
    for w in weights:
        rows, width = w[0].shape[1:]
        out += [pltpu.VMEM((rows, width), BF16),
                pltpu.VMEM((slots, _chunk_rows(width), width), F32),
                pltpu.SemaphoreType.DMA((slots,))]
    return out


def _fetch_weights(hbm, layers, fetched):
    plans = []
    for w_hbm, layer, (w_vmem, stage, sem) in zip(hbm, layers, fetched):
        rows = w_vmem.shape[0]
        slots, r = stage.shape[:2]
        assert rows % r == 0 and rows // r >= slots

        def chunk(c, w_hbm=w_hbm, layer=layer, stage=stage, sem=sem, slots=slots, r=r):
            slot = c % slots
            return pltpu.make_async_copy(w_hbm.at[layer, pl.ds(c * r, r), :], stage.at[slot],
                                         sem.at[slot])

        plans.append((rows // r, slots, r, chunk, w_vmem, stage))

    for _, slots, _, chunk, _, _ in plans:
        for c in range(slots - 1):
            chunk(c).start()
    trips = max(p[0] for p in plans)

    def step(c, carry):
        for n, slots, r, chunk, w_vmem, stage in plans:
            def advance(n=n, slots=slots, r=r, chunk=chunk, w_vmem=w_vmem, stage=stage):
                @pl.when(c + slots - 1 < n)
                def _():
                    chunk(c + slots - 1).start()

                chunk(c).wait()
                w_vmem[pl.ds(pl.multiple_of(c * r, r), r), :] = stage[c % slots].astype(BF16)

            if n == trips:
                advance()
            else:
                pl.when(c < n)(advance)
        return carry

    lax.fori_loop(0, trips, step, 0)


def _with_weights(body, n_in, layers, n_out):
    n_w = len(layers)

    def kernel(*refs):
        ins = refs[:n_in]
        hbm = refs[n_in:n_in + n_w]
        outs = refs[n_in + n_w:n_in + n_w + n_out]
        scratch = refs[n_in + n_w + n_out:]
        fetched = [scratch[3 * k:3 * k + 3] for k in range(n_w)]

        @pl.when(pl.program_id(0) == 0)
        def _():
            _fetch_weights(hbm, layers, fetched)

        body(*ins, *[f[0] for f in fetched], *outs, *scratch[3 * n_w:])

    return kernel


def _row_slabs(tm):
    rows = tm // ROW_SLABS
    return [slice(s * rows, (s + 1) * rows) for s in range(ROW_SLABS)]


def _rms(x, g):
    ms = jnp.mean(x * x, axis=-1, keepdims=True)
    return x * lax.rsqrt(ms + NORM_EPS) * g


def _mlp_residual(x1, g_ref, wup_ref, wdn_ref):
    xn = _rms(x1, g_ref[...]).astype(BF16)
    acc = jnp.zeros_like(x1)
    for c in range(D_FF // FF_CHUNK):
        cs = slice(c * FF_CHUNK, (c + 1) * FF_CHUNK)
        h = jnp.dot(xn, wup_ref[:, cs], preferred_element_type=F32)
        h = jnp.maximum(h, 0.0)
        h = (h * h).astype(BF16)
        acc = acc + jnp.dot(h, wdn_ref[cs, :], preferred_element_type=F32)
    return x1 + acc


def _qkv_kernel(x_ref, g_ref, cos_ref, sin_ref, w_ref, qt_ref, k_ref, vt_ref, km_ref):
    tm = x_ref.shape[0]
    xn = _rms(x_ref[...], g_ref[...]).astype(BF16)
    cos = cos_ref[...]
    sin = sin_ref[...]
    lane = lax.broadcasted_iota(jnp.int32, (tm, HEAD_DIM), 1)
    first_half = lane < (ROT_DIM // 2)

    def rope(t):
        parts = []
        for h in range(N_HEADS):
            th = t[:, h * HEAD_DIM:(h + 1) * HEAD_DIM]
            partner = jnp.where(first_half,
                                pltpu.roll(th, HEAD_DIM - ROT_DIM // 2, 1),
                                pltpu.roll(th, ROT_DIM // 2, 1))
            parts.append(th * cos + partner * sin)
        return jnp.concatenate(parts, axis=1)

    q = jnp.dot(xn, w_ref[:, 0:D_MODEL], preferred_element_type=F32)
    q = rope(q) * Q_SCALE
    for b in range(tm // MOBA_BLOCK):
        qt_ref[b] = q[b * MOBA_BLOCK:(b + 1) * MOBA_BLOCK, :].T.astype(BF16)

    k = jnp.dot(xn, w_ref[:, D_MODEL:2 * D_MODEL], preferred_element_type=F32)
    k = rope(k)
    k_ref[...] = k.astype(BF16)
    for b in range(tm // MOBA_BLOCK):
        km_ref[b] = jnp.mean(k[b * MOBA_BLOCK:(b + 1) * MOBA_BLOCK, :], axis=0, keepdims=True)

    v = jnp.dot(xn, w_ref[:, 2 * D_MODEL:3 * D_MODEL], preferred_element_type=F32)
    for b in range(tm // MOBA_BLOCK):
        vt_ref[b] = v[b * MOBA_BLOCK:(b + 1) * MOBA_BLOCK, :].T.astype(BF16)


def _qkv_call(x, g, w, cos, sin):
    t = x.shape[0]
    tm = WIDE_ROW_TILE
    nb = tm // MOBA_BLOCK
    return pl.pallas_call(
        _with_weights(_qkv_kernel, n_in=4, layers=(w[1],), n_out=4),
        grid=(t // tm,),
        in_specs=[
            pl.BlockSpec((tm, D_MODEL), lambda i: (i, 0)),
            _resident(*g),
            pl.BlockSpec((tm, HEAD_DIM), lambda i: (i, 0)),
            pl.BlockSpec((tm, HEAD_DIM), lambda i: (i, 0)),
            ANY_SPEC,
        ],
        out_specs=[
            pl.BlockSpec((nb, D_MODEL, MOBA_BLOCK), lambda i: (i, 0, 0)),
            pl.BlockSpec((tm, D_MODEL), lambda i: (i, 0)),
            pl.BlockSpec((nb, D_MODEL, MOBA_BLOCK), lambda i: (i, 0, 0)),
            pl.BlockSpec((nb, 1, D_MODEL), lambda i: (i, 0, 0)),
        ],
        out_shape=[
            jax.ShapeDtypeStruct((t // MOBA_BLOCK, D_MODEL, MOBA_BLOCK), BF16),
            jax.ShapeDtypeStruct((t, D_MODEL), BF16),
            jax.ShapeDtypeStruct((t // MOBA_BLOCK, D_MODEL, MOBA_BLOCK), BF16),
            jax.ShapeDtypeStruct((t // MOBA_BLOCK, 1, D_MODEL), F32),
        ],
        scratch_shapes=_weight_scratch((w,)),
        compiler_params=pltpu.CompilerParams(
            dimension_semantics=("arbitrary",), vmem_limit_bytes=VMEM_LIMIT),
        name="moba_qkv",
    )(x, g[0], cos, sin, w[0])


def _attn_kernel(qa_ref, qb_ref, k_ref, vt_ref, km_ref, o_hbm, q2_ref, s_ref, bias_ref, stage_ref,
                 sem, *head_refs, n_blocks):
    heads = N_HEADS
    blk = MOBA_BLOCK
    acc_refs, stat_refs = head_refs[:heads], head_refs[heads:]
    b, p = pl.program_id(0), pl.program_id(1)
    tile_block = (p, n_blocks - 1 - p)
    q_refs = (qa_ref, qb_ref)
    n_past = n_blocks - 1

    def hs(h):
        return slice(h * HEAD_DIM, (h + 1) * HEAD_DIM)

    def key_rows(j):
        return pl.ds(pl.multiple_of(j * blk, blk), blk)

    def step_target(t):
        tile = jnp.where(t >= tile_block[0], 1, 0)
        return tile, t - tile * tile_block[0]

    ones_rows = jnp.ones((ONES_ROWS, blk), BF16)

    def weighted_values(j, h, prob):
        vt_aug = jnp.concatenate([vt_ref[j, hs(h), :], ones_rows], axis=0)
        out = jnp.dot(vt_aug, prob.astype(BF16), preferred_element_type=F32)
        return out[:HEAD_DIM], out[HEAD_DIM:HEAD_DIM + 1]

    for tile in range(2):
        q2_ref[tile] = q_refs[tile][0]

    row = lax.broadcasted_iota(jnp.int32, (n_blocks, blk), 0)
    kpos = lax.broadcasted_iota(jnp.int32, (blk, blk), 0)
    qpos = lax.broadcasted_iota(jnp.int32, (blk, blk), 1)
    causal = kpos <= qpos
    for tile in range(2):
        i = tile_block[tile]
        q_ref = q_refs[tile]
        own = [jnp.dot(k_ref[key_rows(i), hs(h)], q_ref[0, hs(h), :], preferred_element_type=F32)
               for h in range(heads)]
        gates = [jnp.dot(km_ref[:, hs(h)].astype(BF16), q_ref[0, hs(h), :],
                         preferred_element_type=F32) for h in range(heads)]
        valid = row < i
        for h in range(heads):
            gate = jnp.where(valid, gates[h], NEG_INF)
            bias = jnp.full((n_blocks, blk), NEG_INF, F32)
            for _ in range(MOBA_TOPK):
                top = jnp.max(gate, axis=0, keepdims=True)
                first = jnp.min(jnp.where(gate == top, row, n_blocks), axis=0, keepdims=True)
                pick = row == first
                bias = jnp.where(pick, 0.0, bias)
                gate = jnp.where(pick, -jnp.inf, gate)
            bias_ref[tile, h] = jnp.where(valid, bias, NEG_INF)
        for h in range(heads):
            s = jnp.where(causal, own[h], NEG_INF)
            m = jnp.max(s, axis=0, keepdims=True)
            pv, p_sum = weighted_values(i, h, jnp.exp2(s - m))
            acc_refs[h][tile] = pv
            stat_refs[h][tile, 0:1, :] = m
            stat_refs[h][tile, 1:2, :] = p_sum

    def scores(t, h):
        tile, j = step_target(t)
        return jnp.dot(k_ref[key_rows(j), hs(h)], q2_ref[tile, hs(h), :],
                       preferred_element_type=F32)

    def past_block(h, slot, tile, j):
        s = s_ref[slot, h]
        bias = bias_ref[tile, h, pl.ds(j, 1), :]
        m_old = stat_refs[h][tile, 0:1, :]
        m_new = jnp.maximum(m_old, jnp.max(s, axis=0, keepdims=True) + bias)
        alpha = jnp.exp2(m_old - m_new)
        pv, p_sum = weighted_values(j, h, jnp.exp2(s - (m_new - bias)))
        acc_refs[h][tile] = alpha * acc_refs[h][tile] + pv
        stat_refs[h][tile, 0:1, :] = m_new
        stat_refs[h][tile, 1:2, :] = alpha * stat_refs[h][tile, 1:2, :] + p_sum

    for h in range(heads):
        s_ref[0, h] = scores(0, h)
    for t in range(n_past):
        slot = t % 2
        tile, j = step_target(t)
        for h in range(heads):
            past_block(h, slot, tile, j)
            if t + 1 < n_past:
                s_ref[1 - slot, h] = scores(t + 1, h)

    pairs = n_blocks // 2
    step = b * pairs + p
    slot = step % 2

    def out_copy(slot, tile, batch_idx, pair_idx):
        block = pair_idx if tile == 0 else n_blocks - 1 - pair_idx
        return pltpu.make_async_copy(stage_ref.at[slot, tile],
                                     o_hbm.at[batch_idx * n_blocks + block], sem.at[slot, tile])

    for tile in range(2):
        for h in range(heads):
            inv_l = 1.0 / stat_refs[h][tile, 1:2, :]
            stage_ref[slot, tile, hs(h), :] = (acc_refs[h][tile] * inv_l).astype(BF16)
        out_copy(slot, tile, b, p).start()

    @pl.when(step > 0)
    def _():
        prev_b = jnp.where(p == 0, b - 1, b)
        prev_p = jnp.where(p == 0, pairs - 1, p - 1)
        for tile in range(2):
            out_copy(1 - slot, tile, prev_b, prev_p).wait()

    @pl.when(step == pl.num_programs(0) * pairs - 1)
    def _():
        for tile in range(2):
            out_copy(slot, tile, b, p).wait()


def _attn_call(qt, k, vt, km, batch, seq):
    n_blocks = seq // MOBA_BLOCK
    assert n_blocks % 2 == 0
    q_block = (1, D_MODEL, MOBA_BLOCK)
    per_head = ([pltpu.VMEM((2, HEAD_DIM, MOBA_BLOCK), F32)] * N_HEADS
                + [pltpu.VMEM((2, 2, MOBA_BLOCK), F32)] * N_HEADS)
    return pl.pallas_call(
        functools.partial(_attn_kernel, n_blocks=n_blocks),
        grid=(batch, n_blocks // 2),
        in_specs=[
            pl.BlockSpec(q_block, lambda b, p: (b * n_blocks + p, 0, 0)),
            pl.BlockSpec(q_block, lambda b, p: (b * n_blocks + n_blocks - 1 - p, 0, 0)),
            pl.BlockSpec((seq, D_MODEL), lambda b, p: (b, 0)),
            pl.BlockSpec((n_blocks, D_MODEL, MOBA_BLOCK), lambda b, p: (b, 0, 0)),
            pl.BlockSpec((n_blocks, D_MODEL), lambda b, p: (b, 0)),
        ],
        out_specs=ANY_SPEC,
        out_shape=jax.ShapeDtypeStruct((batch * n_blocks, D_MODEL, MOBA_BLOCK), BF16),
        scratch_shapes=[
            pltpu.VMEM((2, D_MODEL, MOBA_BLOCK), BF16),
            pltpu.VMEM((2, N_HEADS, MOBA_BLOCK, MOBA_BLOCK), F32),
            pltpu.VMEM((2, N_HEADS, n_blocks, MOBA_BLOCK), F32),
            pltpu.VMEM((2, 2, D_MODEL, MOBA_BLOCK), BF16),
            pltpu.SemaphoreType.DMA((2, 2)),
        ] + per_head,
        compiler_params=pltpu.CompilerParams(
            dimension_semantics=("arbitrary", "arbitrary"), vmem_limit_bytes=VMEM_LIMIT),
        name="moba_attention",
    )(qt, qt, k, vt, km)


def _attn_mlp_kernel(x_ref, a_ref, g_ref, *rest, final):
    if final:
        gf_ref, wo_ref, wup_ref, wdn_ref, o_ref = rest
    else:
        wo_ref, wup_ref, wdn_ref, o_ref = rest
    slabs = [slice(s * MOBA_BLOCK, (s + 1) * MOBA_BLOCK) for s in range(a_ref.shape[0])]
    lhs_dim0 = (((0,), (0,)), ((), ()))
    x1 = [x_ref[rows, :] + lax.dot_general(a_ref[s], wo_ref[...], lhs_dim0,
                                           preferred_element_type=F32)
          for s, rows in enumerate(slabs)]
    for rows, x1_slab in zip(slabs, x1):
        out = _mlp_residual(x1_slab, g_ref, wup_ref, wdn_ref)
        if final:
            out = _rms(out, gf_ref[...])
        o_ref[rows, :] = out


def _attn_mlp_call(x, a, wo, g, wup, wdn, g_final=None):
    t = x.shape[0]
    tm = WIDE_ROW_TILE
    final = g_final is not None
    in_specs = [
        pl.BlockSpec((tm, D_MODEL), lambda i: (i, 0)),
        pl.BlockSpec((tm // MOBA_BLOCK, D_MODEL, MOBA_BLOCK), lambda i: (i, 0, 0)),
        _resident(*g),
    ]
    args = [x, a, g[0]]
    if final:
        in_specs.append(_resident(*g_final))
        args.append(g_final[0])
    weights = (wo, wup, wdn)
    return pl.pallas_call(
        _with_weights(functools.partial(_attn_mlp_kernel, final=final), n_in=len(args),
                      layers=[w[1] for w in weights], n_out=1),
        grid=(t // tm,),
        in_specs=in_specs + [ANY_SPEC] * len(weights),
        out_specs=pl.BlockSpec((tm, D_MODEL), lambda i: (i, 0)),
        out_shape=jax.ShapeDtypeStruct((t, D_MODEL), F32),
        scratch_shapes=_weight_scratch(weights),
        compiler_params=pltpu.CompilerParams(
            dimension_semantics=("arbitrary",), vmem_limit_bytes=VMEM_LIMIT),
        name="attn_out_mlp",
    )(*args, *[w[0] for w in weights])


def _halo_spec(tm):
    per = tm // HALO
    return pl.BlockSpec((HALO, D_MODEL), lambda i: (jnp.maximum(i * per - 1, 0), 0))


def _pool_mlp_kernel(x_ref, halo_ref, gmix_ref, pw_ref, ps_ref, g_ref, wup_ref, wdn_ref,
                     o_ref, ext_ref, *, tiles_per_seq):
    tm = x_ref.shape[0]
    tile_in_seq = pl.program_id(0) % tiles_per_seq
    x = x_ref[...]
    gmix = gmix_ref[...]
    xn = _rms(x, gmix)
    ext_ref[0:HALO, :] = jnp.where(tile_in_seq == 0, 0.0,
                                   _rms(halo_ref[...], gmix)).astype(BF16)
    ext_ref[HALO:HALO + tm, :] = xn.astype(BF16)
    pos = tile_in_seq * tm + lax.broadcasted_iota(jnp.int32, (tm, 1), 0)
    ext_rows = POOL_ROWS + HALO
    t_idx = lax.broadcasted_iota(jnp.int32, (POOL_ROWS, ext_rows), 0) + HALO
    e_idx = lax.broadcasted_iota(jnp.int32, (POOL_ROWS, ext_rows), 1)
    bands = [jnp.where(e_idx > t_idx - w, jnp.where(e_idx <= t_idx, 1.0, 0.0), 0.0).astype(BF16)
             for w in POOL_WINDOWS]
    slabs = _row_slabs(tm)
    x1 = []
    for rows in slabs:
        ys = []
        for g, w in enumerate(POOL_WINDOWS):
            cs = slice(g * POOL_GROUP_DIM, (g + 1) * POOL_GROUP_DIM)
            total = jnp.concatenate(
                [jnp.dot(bands[g], ext_ref[r0:r0 + ext_rows, cs], preferred_element_type=F32)
                 for r0 in range(rows.start, rows.stop, POOL_ROWS)], axis=0)
            count = jnp.minimum(pos[rows, :] + 1, w).astype(F32)
            pooled = total / count - xn[rows, cs]
            ys.append(jnp.dot(pooled.astype(BF16), pw_ref[g], preferred_element_type=F32))
        x1.append(x[rows, :] + jnp.concatenate(ys, axis=1) * ps_ref[...])
    for rows, x1_slab in zip(slabs, x1):
        o_ref[rows, :] = _mlp_residual(x1_slab, g_ref, wup_ref, wdn_ref)


def _pool_mlp_call(x, gmix, pw, ps, g, wup, wdn, seq):
    t = x.shape[0]
    tm = ROW_TILE
    weights = (wup, wdn)
    return pl.pallas_call(
        _with_weights(functools.partial(_pool_mlp_kernel, tiles_per_seq=seq // tm), n_in=6,
                      layers=[w[1] for w in weights], n_out=1),
        grid=(t // tm,),
        in_specs=[
            pl.BlockSpec((tm, D_MODEL), lambda i: (i, 0)),
            _halo_spec(tm),
            _resident(*gmix),
            _resident(*pw),
            _resident(*ps),
            _resident(*g),
        ] + [ANY_SPEC] * len(weights),
        out_specs=pl.BlockSpec((tm, D_MODEL), lambda i: (i, 0)),
        out_shape=jax.ShapeDtypeStruct((t, D_MODEL), F32),
        scratch_shapes=_weight_scratch(weights) + [pltpu.VMEM((HALO + tm, D_MODEL), BF16)],
        compiler_params=pltpu.CompilerParams(
            dimension_semantics=("arbitrary",), vmem_limit_bytes=VMEM_LIMIT),
        name="pool_mlp",
    )(x, x, gmix[0], pw[0], ps[0], g[0], *[w[0] for w in weights])


def _conv_mlp_kernel(x_ref, halo_ref, gmix_ref, cw_ref, g_ref, win_ref, wout_ref, wup_ref,
                     wdn_ref, o_ref, xe_ref, z_ref, *, tiles_per_seq):
    tm = x_ref.shape[0]
    tile_in_seq = pl.program_id(0) % tiles_per_seq
    x = x_ref[...]
    gmix = gmix_ref[...]
    xe_ref[0:HALO, :] = _rms(halo_ref[...], gmix).astype(BF16)
    xe_ref[HALO:HALO + tm, :] = _rms(x, gmix).astype(BF16)
    gate_c = jnp.dot(xe_ref[...], win_ref[:, D_MODEL:2 * D_MODEL], preferred_element_type=F32)
    hid = jnp.dot(xe_ref[...], win_ref[:, 2 * D_MODEL:3 * D_MODEL], preferred_element_type=F32)
    z = gate_c * hid
    z_ref[0:HALO, :] = jnp.where(tile_in_seq == 0, 0.0, z[0:HALO, :])
    z_ref[HALO:HALO + tm, :] = z[HALO:HALO + tm, :]

    cw = cw_ref[...]
    slabs = _row_slabs(tm)
    x1 = []
    for rows in slabs:
        zc = cw[CONV_WIDTH - 1:CONV_WIDTH, :] * z_ref[HALO + rows.start:HALO + rows.stop, :]
        for back in range(1, CONV_WIDTH):
            tap = cw[CONV_WIDTH - 1 - back:CONV_WIDTH - back, :]
            zc = zc + tap * z_ref[HALO - back + rows.start:HALO - back + rows.stop, :]
        gate_b = jnp.dot(xe_ref[HALO + rows.start:HALO + rows.stop, :], win_ref[:, 0:D_MODEL],
                         preferred_element_type=F32)
        y = jnp.dot((gate_b * zc).astype(BF16), wout_ref[...], preferred_element_type=F32)
        x1.append(x[rows, :] + y)
    for rows, x1_slab in zip(slabs, x1):
        o_ref[rows, :] = _mlp_residual(x1_slab, g_ref, wup_ref, wdn_ref)


def _conv_mlp_call(x, gmix, win, cw, wout, g, wup, wdn, seq):
    t = x.shape[0]
    tm = ROW_TILE
    weights = (win, wout, wup, wdn)
    return pl.pallas_call(
        _with_weights(functools.partial(_conv_mlp_kernel, tiles_per_seq=seq // tm), n_in=5,
                      layers=[w[1] for w in weights], n_out=1),
        grid=(t // tm,),
        in_specs=[
            pl.BlockSpec((tm, D_MODEL), lambda i: (i, 0)),
            _halo_spec(tm),
            _resident(*gmix),
            _resident(*cw),
            _resident(*g),
        ] + [ANY_SPEC] * len(weights),
        out_specs=pl.BlockSpec((tm, D_MODEL), lambda i: (i, 0)),
        out_shape=jax.ShapeDtypeStruct((t, D_MODEL), F32),
        scratch_shapes=_weight_scratch(weights) + [
            pltpu.VMEM((HALO + tm, D_MODEL), BF16),
            pltpu.VMEM((HALO + tm, D_MODEL), F32),
        ],
        compiler_params=pltpu.CompilerParams(
            dimension_semantics=("arbitrary",), vmem_limit_bytes=VMEM_LIMIT),
        name="conv_mlp",
    )(x, x, gmix[0], cw[0], g[0], *[w[0] for w in weights])


ROPE_PACK = HEAD_DIM // ROT_DIM


def _rope_table_kernel(pos_ref, freq_ref, sign_ref, cos_ref, sin_ref):
    ang = pos_ref[...].astype(F32) * freq_ref[...]
    c = jnp.cos(ang)
    s = jnp.sin(ang) * sign_ref[...]
    rotary = lax.broadcasted_iota(jnp.int32, c.shape, 1) < ROT_DIM
    for g in range(ROPE_PACK):
        shift = (HEAD_DIM - ROT_DIM * g) % HEAD_DIM
        cg = pltpu.roll(c, shift, 1) if shift else c
        sg = pltpu.roll(s, shift, 1) if shift else s
        cos_ref[g] = jnp.where(rotary, cg, 1.0)
        sin_ref[g] = jnp.where(rotary, sg, 0.0)


def _rope_tables(positions):
    n = positions.size
    rows = n // ROPE_PACK
    half = ROT_DIM // 2
    inv_freq = ROPE_THETA ** (-jnp.arange(0, ROT_DIM, 2, dtype=F32) / ROT_DIM)
    freq = jnp.tile(inv_freq, HEAD_DIM // half).reshape(1, HEAD_DIM)
    sign = jnp.tile(jnp.concatenate([-jnp.ones((half,), F32), jnp.ones((half,), F32)]),
                    ROPE_PACK).reshape(1, HEAD_DIM)
    packed = jnp.repeat(positions.reshape(ROPE_PACK, rows).T, ROT_DIM, axis=1)
    rt = ROW_TILE
    table = jax.ShapeDtypeStruct((ROPE_PACK, rows, HEAD_DIM), F32)
    row_spec = pl.BlockSpec((1, HEAD_DIM), lambda i: (0, 0))
    cos, sin = pl.pallas_call(
        _rope_table_kernel,
        grid=(rows // rt,),
        in_specs=[pl.BlockSpec((rt, HEAD_DIM), lambda i: (i, 0)), row_spec, row_spec],
        out_specs=[pl.BlockSpec((ROPE_PACK, rt, HEAD_DIM), lambda i: (0, i, 0))] * 2,
        out_shape=[table, table],
        compiler_params=pltpu.CompilerParams(dimension_semantics=("arbitrary",)),
        name="rope_tables",
    )(packed, freq, sign)
    return cos.reshape(n, HEAD_DIM), sin.reshape(n, HEAD_DIM)


def kernel(x, positions, norm_mix, norm_mlp, attn_w_qkv, attn_w_o, pool_w, pool_scale,
           conv_w_in, conv_w, conv_w_out, mlp_w_up, mlp_w_down, norm_final):
    batch, seq, d = x.shape
    depth = norm_mix.shape[0]
    assert d == D_MODEL and seq % ROW_TILE == 0 and ROW_TILE % MOBA_BLOCK == 0
    assert (depth - 1) % 3 == 0, "the final norm is fused into a MoBA layer's MLP stage"
    cos, sin = _rope_tables(positions)
    h = x.reshape(batch * seq, d)
    rows = lambda v: v.reshape(v.shape[0], 1, v.shape[-1])
    g_mix, g_mlp, p_scale = rows(norm_mix), rows(norm_mlp), rows(pool_scale)
    g_final = norm_final.reshape(1, 1, d)
    w_pool = pool_w.astype(BF16)
    i_attn = i_pool = i_conv = 0
    for i in range(depth):
        kind = i % 3
        mlp = ((g_mlp, i), (mlp_w_up, i), (mlp_w_down, i))
        if kind == 0:
            qt, k, vt, km = _qkv_call(h, (g_mix, i), (attn_w_qkv, i_attn), cos, sin)
            a = _attn_call(qt, k, vt, km.reshape(-1, d), batch, seq)
            h = _attn_mlp_call(h, a, (attn_w_o, i_attn), *mlp,
                               g_final=(g_final, 0) if i == depth - 1 else None)
            i_attn += 1
        elif kind == 1:
            h = _pool_mlp_call(h, (g_mix, i), (w_pool, i_pool), (p_scale, i_pool), *mlp, seq)
            i_pool += 1
        else:
            h = _conv_mlp_call(h, (g_mix, i), (conv_w_in, i_conv), (conv_w, i_conv),
                               (conv_w_out, i_conv), *mlp, seq)
            i_conv += 1
    return h.reshape(batch, seq, d)
```

```python
import functools
import math

import jax
import jax.numpy as jnp
from jax import lax
from jax.experimental import pallas as pl
from jax.experimental.pallas import tpu as pltpu

D_MODEL = 1024
N_HEADS = 8
HEAD_DIM = D_MODEL // N_HEADS
ROT_DIM = HEAD_DIM // 4
ROPE_THETA = 500000.0
MOBA_BLOCK = 256
MOBA_TOPK = 3
POOL_WINDOWS = (2, 4, 8, 16)
POOL_GROUP_DIM = D_MODEL // len(POOL_WINDOWS)
CONV_WIDTH = 3
D_FF = 4 * D_MODEL
NORM_EPS = 1e-6
NEG_INF = -1e30

HALO = 16
POOL_ROWS = 128
ROW_SLABS = 2
ROW_TILE = 512
WIDE_ROW_TILE = 1024
FF_CHUNK = 1024
ONES_ROWS = 16
STAGE_BYTES = 512 * 1024
IN_FLIGHT_BYTES = 4 * 1024 * 1024
ANY_SPEC = pl.BlockSpec(memory_space=pl.ANY)
VMEM_LIMIT = 56 * 1024 * 1024
Q_SCALE = (1.0 / math.sqrt(HEAD_DIM)) * math.log2(math.e)

F32 = jnp.float32
BF16 = jnp.bfloat16


def _resident(stacked, layer):
    tail = stacked.shape[1:]
    index = (layer,) + (0,) * len(tail)
    return pl.BlockSpec((None,) + tail, lambda *_: index, pipeline_mode=pl.Buffered(1))


def _chunk_rows(width):
    rows = STAGE_BYTES // (4 * width)
    return rows - rows % 16


def _stage_slots(n_weights):
    return -(-IN_FLIGHT_BYTES // (n_weights * STAGE_BYTES)) + 1


def _weight_scratch(weights):
    slots = _stage_slots(len(weights))
    out = []
    for w in weights:
        rows, width = w[0].shape[1:]
        out += [pltpu.VMEM((rows, width), BF16),
                pltpu.VMEM((slots, _chunk_rows(width), width), F32),
                pltpu.SemaphoreType.DMA((slots,))]
    return out


def _fetch_weights(hbm, layers, fetched):
    plans = []
    for w_hbm, layer, (w_vmem, stage, sem) in zip(hbm, layers, fetched):
        rows = w_vmem.shape[0]
        slots, r = stage.shape[:2]
        assert rows % r == 0 and rows // r >= slots

        def chunk(c, w_hbm=w_hbm, layer=layer, stage=stage, sem=sem, slots=slots, r=r):
            slot = c % slots
            return pltpu.make_async_copy(w_hbm.at[layer, pl.ds(c * r, r), :], stage.at[slot],
                                         sem.at[slot])

        plans.append((rows // r, slots, r, chunk, w_vmem, stage))

    for _, slots, _, chunk, _, _ in plans:
        for c in range(slots - 1):
            chunk(c).start()
    trips = max(p[0] for p in plans)

    def step(c, carry):
        for n, slots, r, chunk, w_vmem, stage in plans:
            def advance(n=n, slots=slots, r=r, chunk=chunk, w_vmem=w_vmem, stage=stage):
                @pl.when(c + slots - 1 < n)
                def _():
                    chunk(c + slots - 1).start()

                chunk(c).wait()
                w_vmem[pl.ds(pl.multiple_of(c * r, r), r), :] = stage[c % slots].astype(BF16)

            if n == trips:
                advance()
            else:
                pl.when(c < n)(advance)
        return carry

    lax.fori_loop(0, trips, step, 0)


def _with_weights(body, n_in, layers, n_out):
    n_w = len(layers)

    def kernel(*refs):
        ins = refs[:n_in]
        hbm = refs[n_in:n_in + n_w]
        outs = refs[n_in + n_w:n_in + n_w + n_out]
        scratch = refs[n_in + n_w + n_out:]
        fetched = [scratch[3 * k:3 * k + 3] for k in range(n_w)]

        @pl.when(pl.program_id(0) == 0)
        def _():
            _fetch_weights(hbm, layers, fetched)

        body(*ins, *[f[0] for f in fetched], *outs, *scratch[3 * n_w:])

    return kernel


def _row_slabs(tm):
    rows = tm // ROW_SLABS
    return [slice(s * rows, (s + 1) * rows) for s in range(ROW_SLABS)]


def _rms(x, g):
    ms = jnp.mean(x * x, axis=-1, keepdims=True)
    return x * lax.rsqrt(ms + NORM_EPS) * g


def _mlp_residual(x1, g_ref, wup_ref, wdn_ref):
    xn = _rms(x1, g_ref[...]).astype(BF16)
    acc = jnp.zeros_like(x1)
    for c in range(D_FF // FF_CHUNK):
        cs = slice(c * FF_CHUNK, (c + 1) * FF_CHUNK)
        h = jnp.dot(xn, wup_ref[:, cs], preferred_element_type=F32)
        h = jnp.maximum(h, 0.0)
        h = (h * h).astype(BF16)
        acc = acc + jnp.dot(h, wdn_ref[cs, :], preferred_element_type=F32)
    return x1 + acc


def _qkv_kernel(x_ref, g_ref, cos_ref, sin_ref, w_ref, qt_ref, k_ref, vt_ref, km_ref):
    tm = x_ref.shape[0]
    xn = _rms(x_ref[...], g_ref[...]).astype(BF16)
    cos = cos_ref[...]
    sin = sin_ref[...]
    lane = lax.broadcasted_iota(jnp.int32, (tm, HEAD_DIM), 1)
    first_half = lane < (ROT_DIM // 2)

    def rope(t):
        parts = []
        for h in range(N_HEADS):
            th = t[:, h * HEAD_DIM:(h + 1) * HEAD_DIM]
            partner = jnp.where(first_half,
                                pltpu.roll(th, HEAD_DIM - ROT_DIM // 2, 1),
                                pltpu.roll(th, ROT_DIM // 2, 1))
            parts.append(th * cos + partner * sin)
        return jnp.concatenate(parts, axis=1)

    q = jnp.dot(xn, w_ref[:, 0:D_MODEL], preferred_element_type=F32)
    q = rope(q) * Q_SCALE
    for b in range(tm // MOBA_BLOCK):
        qt_ref[b] = q[b * MOBA_BLOCK:(b + 1) * MOBA_BLOCK, :].T.astype(BF16)

    k = jnp.dot(xn, w_ref[:, D_MODEL:2 * D_MODEL], preferred_element_type=F32)
    k = rope(k)
    k_ref[...] = pltpu.bitcast(k.astype(BF16), jnp.uint32)
    for b in range(tm // MOBA_BLOCK):
        km_ref[b] = jnp.mean(k[b * MOBA_BLOCK:(b + 1) * MOBA_BLOCK, :], axis=0, keepdims=True)

    v = jnp.dot(xn, w_ref[:, 2 * D_MODEL:3 * D_MODEL], preferred_element_type=F32)
    for b in range(tm // MOBA_BLOCK):
        vt_ref[b] = v[b * MOBA_BLOCK:(b + 1) * MOBA_BLOCK, :].T.astype(BF16)


def _qkv_call(x, g, w, cos, sin):
    t = x.shape[0]
    tm = WIDE_ROW_TILE
    nb = tm // MOBA_BLOCK
    return pl.pallas_call(
        _with_weights(_qkv_kernel, n_in=4, layers=(w[1],), n_out=4),
        grid=(t // tm,),
        in_specs=[
            pl.BlockSpec((tm, D_MODEL), lambda i: (i, 0)),
            _resident(*g),
            pl.BlockSpec((tm, HEAD_DIM), lambda i: (i, 0)),
            pl.BlockSpec((tm, HEAD_DIM), lambda i: (i, 0)),
            ANY_SPEC,
        ],
        out_specs=[
            pl.BlockSpec((nb, D_MODEL, MOBA_BLOCK), lambda i: (i, 0, 0)),
            pl.BlockSpec((tm // 2, D_MODEL), lambda i: (i, 0)),
            pl.BlockSpec((nb, D_MODEL, MOBA_BLOCK), lambda i: (i, 0, 0)),
            pl.BlockSpec((nb, 1, D_MODEL), lambda i: (i, 0, 0)),
        ],
        out_shape=[
            jax.ShapeDtypeStruct((t // MOBA_BLOCK, D_MODEL, MOBA_BLOCK), BF16),
            jax.ShapeDtypeStruct((t // 2, D_MODEL), jnp.uint32),
            jax.ShapeDtypeStruct((t // MOBA_BLOCK, D_MODEL, MOBA_BLOCK), BF16),
            jax.ShapeDtypeStruct((t // MOBA_BLOCK, 1, D_MODEL), F32),
        ],
        scratch_shapes=_weight_scratch((w,)),
        compiler_params=pltpu.CompilerParams(
            dimension_semantics=("arbitrary",), vmem_limit_bytes=VMEM_LIMIT),
        name="moba_qkv",
    )(x, g[0], cos, sin, w[0])


def _attn_kernel(qa_ref, qb_ref, k_ref, vt_ref, km_ref, o_hbm, q2_ref, s_ref, bias_ref, stage_ref,
                 sem, *head_refs, n_blocks):
    heads = N_HEADS
    blk = MOBA_BLOCK
    acc_refs, stat_refs = head_refs[:heads], head_refs[heads:]
    b, p = pl.program_id(0), pl.program_id(1)
    tile_block = (p, n_blocks - 1 - p)
    q_refs = (qa_ref, qb_ref)
    n_past = n_blocks - 1

    def hs(h):
        return slice(h * HEAD_DIM, (h + 1) * HEAD_DIM)

    def keys(j, h):
        rows = pl.ds(pl.multiple_of(j * (blk // 2), blk // 2), blk // 2)
        return pltpu.bitcast(k_ref[rows, hs(h)], BF16)

    def step_target(t):
        tile = jnp.where(t >= tile_block[0], 1, 0)
        return tile, t - tile * tile_block[0]

    ones_rows = jnp.ones((ONES_ROWS, blk), BF16)

    def weighted_values(j, h, prob):
        vt_aug = jnp.concatenate([vt_ref[j, hs(h), :], ones_rows], axis=0)
        out = jnp.dot(vt_aug, prob.astype(BF16), preferred_element_type=F32)
        return out[:HEAD_DIM], out[HEAD_DIM:HEAD_DIM + 1]

    for tile in range(2):
        q2_ref[tile] = q_refs[tile][0]

    row = lax.broadcasted_iota(jnp.int32, (n_blocks, blk), 0)
    kpos = lax.broadcasted_iota(jnp.int32, (blk, blk), 0)
    qpos = lax.broadcasted_iota(jnp.int32, (blk, blk), 1)
    causal = kpos <= qpos
    for tile in range(2):
        i = tile_block[tile]
        q_ref = q_refs[tile]
        own = [jnp.dot(keys(i, h), q_ref[0, hs(h), :], preferred_element_type=F32)
               for h in range(heads)]
        gates = [jnp.dot(km_ref[:, hs(h)].astype(BF16), q_ref[0, hs(h), :],
                         preferred_element_type=F32) for h in range(heads)]
        valid = row < i
        for h in range(heads):
            gate = jnp.where(valid, gates[h], NEG_INF)
            bias = jnp.full((n_blocks, blk), NEG_INF, F32)
            for _ in range(MOBA_TOPK):
                top = jnp.max(gate, axis=0, keepdims=True)
                first = jnp.min(jnp.where(gate == top, row, n_blocks), axis=0, keepdims=True)
                pick = row == first
                bias = jnp.where(pick, 0.0, bias)
                gate = jnp.where(pick, -jnp.inf, gate)
            bias_ref[tile, h] = jnp.where(valid, bias, NEG_INF)
        for h in range(heads):
            s = jnp.where(causal, own[h], NEG_INF)
            m = jnp.max(s, axis=0, keepdims=True)
            pv, p_sum = weighted_values(i, h, jnp.exp2(s - m))
            acc_refs[h][tile] = pv
            stat_refs[h][tile, 0:1, :] = m
            stat_refs[h][tile, 1:2, :] = p_sum

    def scores(t, h):
        tile, j = step_target(t)
        return jnp.dot(keys(j, h), q2_ref[tile, hs(h), :],
                       preferred_element_type=F32)

    def past_block(h, slot, tile, j):
        s = s_ref[slot, h]
        bias = bias_ref[tile, h, pl.ds(j, 1), :]
        m_old = stat_refs[h][tile, 0:1, :]
        m_new = jnp.maximum(m_old, jnp.max(s, axis=0, keepdims=True) + bias)
        alpha = jnp.exp2(m_old - m_new)
        pv, p_sum = weighted_values(j, h, jnp.exp2(s - (m_new - bias)))
        acc_refs[h][tile] = alpha * acc_refs[h][tile] + pv
        stat_refs[h][tile, 0:1, :] = m_new
        stat_refs[h][tile, 1:2, :] = alpha * stat_refs[h][tile, 1:2, :] + p_sum

    for h in range(heads):
        s_ref[0, h] = scores(0, h)
    for t in range(n_past):
        slot = t % 2
        tile, j = step_target(t)
        for h in range(heads):
            past_block(h, slot, tile, j)
            if t + 1 < n_past:
                s_ref[1 - slot, h] = scores(t + 1, h)

    pairs = n_blocks // 2
    step = b * pairs + p
    slot = step % 2

    def out_copy(slot, tile, batch_idx, pair_idx):
        block = pair_idx if tile == 0 else n_blocks - 1 - pair_idx
        return pltpu.make_async_copy(stage_ref.at[slot, tile],
                                     o_hbm.at[batch_idx * n_blocks + block], sem.at[slot, tile])

    for tile in range(2):
        for h in range(heads):
            inv_l = 1.0 / stat_refs[h][tile, 1:2, :]
            stage_ref[slot, tile, hs(h), :] = (acc_refs[h][tile] * inv_l).astype(BF16)
        out_copy(slot, tile, b, p).start()

    @pl.when(step > 0)
    def _():
        prev_b = jnp.where(p == 0, b - 1, b)
        prev_p = jnp.where(p == 0, pairs - 1, p - 1)
        for tile in range(2):
            out_copy(1 - slot, tile, prev_b, prev_p).wait()

    @pl.when(step == pl.num_programs(0) * pairs - 1)
    def _():
        for tile in range(2):
            out_copy(slot, tile, b, p).wait()


def _attn_call(qt, k, vt, km, batch, seq):
    n_blocks = seq // MOBA_BLOCK
    assert n_blocks % 2 == 0
    q_block = (1, D_MODEL, MOBA_BLOCK)
    per_head = ([pltpu.VMEM((2, HEAD_DIM, MOBA_BLOCK), F32)] * N_HEADS
                + [pltpu.VMEM((2, 2, MOBA_BLOCK), F32)] * N_HEADS)
    return pl.pallas_call(
        functools.partial(_attn_kernel, n_blocks=n_blocks),
        grid=(batch, n_blocks // 2),
        in_specs=[
            pl.BlockSpec(q_block, lambda b, p: (b * n_blocks + p, 0, 0)),
            pl.BlockSpec(q_block, lambda b, p: (b * n_blocks + n_blocks - 1 - p, 0, 0)),
            pl.BlockSpec((seq // 2, D_MODEL), lambda b, p: (b, 0)),
            pl.BlockSpec((n_blocks, D_MODEL, MOBA_BLOCK), lambda b, p: (b, 0, 0)),
            pl.BlockSpec((n_blocks, D_MODEL), lambda b, p: (b, 0)),
        ],
        out_specs=ANY_SPEC,
        out_shape=jax.ShapeDtypeStruct((batch * n_blocks, D_MODEL, MOBA_BLOCK), BF16),
        scratch_shapes=[
            pltpu.VMEM((2, D_MODEL, MOBA_BLOCK), BF16),
            pltpu.VMEM((2, N_HEADS, MOBA_BLOCK, MOBA_BLOCK), F32),
            pltpu.VMEM((2, N_HEADS, n_blocks, MOBA_BLOCK), F32),
            pltpu.VMEM((2, 2, D_MODEL, MOBA_BLOCK), BF16),
            pltpu.SemaphoreType.DMA((2, 2)),
        ] + per_head,
        compiler_params=pltpu.CompilerParams(
            dimension_semantics=("arbitrary", "arbitrary"), vmem_limit_bytes=VMEM_LIMIT),
        name="moba_attention",
    )(qt, qt, k, vt, km)


def _attn_mlp_kernel(x_ref, a_ref, g_ref, *rest, final):
    if final:
        gf_ref, wo_ref, wup_ref, wdn_ref, o_ref = rest
    else:
        wo_ref, wup_ref, wdn_ref, o_ref = rest
    slabs = [slice(s * MOBA_BLOCK, (s + 1) * MOBA_BLOCK) for s in range(a_ref.shape[0])]
    lhs_dim0 = (((0,), (0,)), ((), ()))
    x1 = [x_ref[rows, :] + lax.dot_general(a_ref[s], wo_ref[...], lhs_dim0,
                                           preferred_element_type=F32)
          for s, rows in enumerate(slabs)]
    for rows, x1_slab in zip(slabs, x1):
        out = _mlp_residual(x1_slab, g_ref, wup_ref, wdn_ref)
        if final:
            out = _rms(out, gf_ref[...])
        o_ref[rows, :] = out


def _attn_mlp_call(x, a, wo, g, wup, wdn, g_final=None):
    t = x.shape[0]
    tm = WIDE_ROW_TILE
    final = g_final is not None
    in_specs = [
        pl.BlockSpec((tm, D_MODEL), lambda i: (i, 0)),
        pl.BlockSpec((tm // MOBA_BLOCK, D_MODEL, MOBA_BLOCK), lambda i: (i, 0, 0)),
        _resident(*g),
    ]
    args = [x, a, g[0]]
    if final:
        in_specs.append(_resident(*g_final))
        args.append(g_final[0])
    weights = (wo, wup, wdn)
    return pl.pallas_call(
        _with_weights(functools.partial(_attn_mlp_kernel, final=final), n_in=len(args),
                      layers=[w[1] for w in weights], n_out=1),
        grid=(t // tm,),
        in_specs=in_specs + [ANY_SPEC] * len(weights),
        out_specs=pl.BlockSpec((tm, D_MODEL), lambda i: (i, 0)),
        out_shape=jax.ShapeDtypeStruct((t, D_MODEL), F32),
        scratch_shapes=_weight_scratch(weights),
        compiler_params=pltpu.CompilerParams(
            dimension_semantics=("arbitrary",), vmem_limit_bytes=VMEM_LIMIT),
        name="attn_out_mlp",
    )(*args, *[w[0] for w in weights])


def _halo_spec(tm):
    per = tm // HALO
    return pl.BlockSpec((HALO, D_MODEL), lambda i: (jnp.maximum(i * per - 1, 0), 0))


def _pool_mlp_kernel(x_ref, halo_ref, gmix_ref, pw_ref, ps_ref, g_ref, wup_ref, wdn_ref,
                     o_ref, ext_ref, *, tiles_per_seq):
    tm = x_ref.shape[0]
    tile_in_seq = pl.program_id(0) % tiles_per_seq
    x = x_ref[...]
    gmix = gmix_ref[...]
    xn = _rms(x, gmix)
    ext_ref[0:HALO, :] = jnp.where(tile_in_seq == 0, 0.0,
                                   _rms(halo_ref[...], gmix)).astype(BF16)
    ext_ref[HALO:HALO + tm, :] = xn.astype(BF16)
    pos = tile_in_seq * tm + lax.broadcasted_iota(jnp.int32, (tm, 1), 0)
    ext_rows = POOL_ROWS + HALO
    t_idx = lax.broadcasted_iota(jnp.int32, (POOL_ROWS, ext_rows), 0) + HALO
    e_idx = lax.broadcasted_iota(jnp.int32, (POOL_ROWS, ext_rows), 1)
    bands = [jnp.where(e_idx > t_idx - w, jnp.where(e_idx <= t_idx, 1.0, 0.0), 0.0).astype(BF16)
             for w in POOL_WINDOWS]
    slabs = _row_slabs(tm)
    x1 = []
    for rows in slabs:
        ys = []
        for g, w in enumerate(POOL_WINDOWS):
            cs = slice(g * POOL_GROUP_DIM, (g + 1) * POOL_GROUP_DIM)
            total = jnp.concatenate(
                [jnp.dot(bands[g], ext_ref[r0:r0 + ext_rows, cs], preferred_element_type=F32)
                 for r0 in range(rows.start, rows.stop, POOL_ROWS)], axis=0)
            count = jnp.minimum(pos[rows, :] + 1, w).astype(F32)
            pooled = total / count - xn[rows, cs]
            ys.append(jnp.dot(pooled.astype(BF16), pw_ref[g], preferred_element_type=F32))
        x1.append(x[rows, :] + jnp.concatenate(ys, axis=1) * ps_ref[...])
    for rows, x1_slab in zip(slabs, x1):
        o_ref[rows, :] = _mlp_residual(x1_slab, g_ref, wup_ref, wdn_ref)


def _pool_mlp_call(x, gmix, pw, ps, g, wup, wdn, seq):
    t = x.shape[0]
    tm = ROW_TILE
    weights = (wup, wdn)
    return pl.pallas_call(
        _with_weights(functools.partial(_pool_mlp_kernel, tiles_per_seq=seq // tm), n_in=6,
                      layers=[w[1] for w in weights], n_out=1),
        grid=(t // tm,),
        in_specs=[
            pl.BlockSpec((tm, D_MODEL), lambda i: (i, 0)),
            _halo_spec(tm),
            _resident(*gmix),
            _resident(*pw),
            _resident(*ps),
            _resident(*g),
        ] + [ANY_SPEC] * len(weights),
        out_specs=pl.BlockSpec((tm, D_MODEL), lambda i: (i, 0)),
        out_shape=jax.ShapeDtypeStruct((t, D_MODEL), F32),
        scratch_shapes=_weight_scratch(weights) + [pltpu.VMEM((HALO + tm, D_MODEL), BF16)],
        compiler_params=pltpu.CompilerParams(
            dimension_semantics=("arbitrary",), vmem_limit_bytes=VMEM_LIMIT),
        name="pool_mlp",
    )(x, x, gmix[0], pw[0], ps[0], g[0], *[w[0] for w in weights])


def _conv_mlp_kernel(x_ref, halo_ref, gmix_ref, cw_ref, g_ref, win_ref, wout_ref, wup_ref,
                     wdn_ref, o_ref, xe_ref, z_ref, *, tiles_per_seq):
    tm = x_ref.shape[0]
    tile_in_seq = pl.program_id(0) % tiles_per_seq
    x = x_ref[...]
    gmix = gmix_ref[...]
    xe_ref[0:HALO, :] = _rms(halo_ref[...], gmix).astype(BF16)
    xe_ref[HALO:HALO + tm, :] = _rms(x, gmix).astype(BF16)
    gate_c = jnp.dot(xe_ref[...], win_ref[:, D_MODEL:2 * D_MODEL], preferred_element_type=F32)
    hid = jnp.dot(xe_ref[...], win_ref[:, 2 * D_MODEL:3 * D_MODEL], preferred_element_type=F32)
    z = gate_c * hid
    z_ref[0:HALO, :] = jnp.where(tile_in_seq == 0, 0.0, z[0:HALO, :])
    z_ref[HALO:HALO + tm, :] = z[HALO:HALO + tm, :]

    cw = cw_ref[...]
    slabs = _row_slabs(tm)
    x1 = []
    for rows in slabs:
        zc = cw[CONV_WIDTH - 1:CONV_WIDTH, :] * z_ref[HALO + rows.start:HALO + rows.stop, :]
        for back in range(1, CONV_WIDTH):
            tap = cw[CONV_WIDTH - 1 - back:CONV_WIDTH - back, :]
            zc = zc + tap * z_ref[HALO - back + rows.start:HALO - back + rows.stop, :]
        gate_b = jnp.dot(xe_ref[HALO + rows.start:HALO + rows.stop, :], win_ref[:, 0:D_MODEL],
                         preferred_element_type=F32)
        y = jnp.dot((gate_b * zc).astype(BF16), wout_ref[...], preferred_element_type=F32)
        x1.append(x[rows, :] + y)
    for rows, x1_slab in zip(slabs, x1):
        o_ref[rows, :] = _mlp_residual(x1_slab, g_ref, wup_ref, wdn_ref)


def _conv_mlp_call(x, gmix, win, cw, wout, g, wup, wdn, seq):
    t = x.shape[0]
    tm = ROW_TILE
    weights = (win, wout, wup, wdn)
    return pl.pallas_call(
        _with_weights(functools.partial(_conv_mlp_kernel, tiles_per_seq=seq // tm), n_in=5,
                      layers=[w[1] for w in weights], n_out=1),
        grid=(t // tm,),
        in_specs=[
            pl.BlockSpec((tm, D_MODEL), lambda i: (i, 0)),
            _halo_spec(tm),
            _resident(*gmix),
            _resident(*cw),
            _resident(*g),
        ] + [ANY_SPEC] * len(weights),
        out_specs=pl.BlockSpec((tm, D_MODEL), lambda i: (i, 0)),
        out_shape=jax.ShapeDtypeStruct((t, D_MODEL), F32),
        scratch_shapes=_weight_scratch(weights) + [
            pltpu.VMEM((HALO + tm, D_MODEL), BF16),
            pltpu.VMEM((HALO + tm, D_MODEL), F32),
        ],
        compiler_params=pltpu.CompilerParams(
            dimension_semantics=("arbitrary",), vmem_limit_bytes=VMEM_LIMIT),
        name="conv_mlp",
    )(x, x, gmix[0], cw[0], g[0], *[w[0] for w in weights])


ROPE_PACK = HEAD_DIM // ROT_DIM


def _rope_table_kernel(pos_ref, freq_ref, sign_ref, cos_ref, sin_ref):
    ang = pos_ref[...].astype(F32) * freq_ref[...]
    c = jnp.cos(ang)
    s = jnp.sin(ang) * sign_ref[...]
    rotary = lax.broadcasted_iota(jnp.int32, c.shape, 1) < ROT_DIM
    for g in range(ROPE_PACK):
        shift = (HEAD_DIM - ROT_DIM * g) % HEAD_DIM
        cg = pltpu.roll(c, shift, 1) if shift else c
        sg = pltpu.roll(s, shift, 1) if shift else s
        cos_ref[g] = jnp.where(rotary, cg, 1.0)
        sin_ref[g] = jnp.where(rotary, sg, 0.0)


def _rope_tables(positions):
    n = positions.size
    rows = n // ROPE_PACK
    half = ROT_DIM // 2
    inv_freq = ROPE_THETA ** (-jnp.arange(0, ROT_DIM, 2, dtype=F32) / ROT_DIM)
    freq = jnp.tile(inv_freq, HEAD_DIM // half).reshape(1, HEAD_DIM)
    sign = jnp.tile(jnp.concatenate([-jnp.ones((half,), F32), jnp.ones((half,), F32)]),
                    ROPE_PACK).reshape(1, HEAD_DIM)
    packed = jnp.repeat(positions.reshape(ROPE_PACK, rows).T, ROT_DIM, axis=1)
    rt = ROW_TILE
    table = jax.ShapeDtypeStruct((ROPE_PACK, rows, HEAD_DIM), F32)
    row_spec = pl.BlockSpec((1, HEAD_DIM), lambda i: (0, 0))
    cos, sin = pl.pallas_call(
        _rope_table_kernel,
        grid=(rows // rt,),
        in_specs=[pl.BlockSpec((rt, HEAD_DIM), lambda i: (i, 0)), row_spec, row_spec],
        out_specs=[pl.BlockSpec((ROPE_PACK, rt, HEAD_DIM), lambda i: (0, i, 0))] * 2,
        out_shape=[table, table],
        compiler_params=pltpu.CompilerParams(dimension_semantics=("arbitrary",)),
        name="rope_tables",
    )(packed, freq, sign)
    return cos.reshape(n, HEAD_DIM), sin.reshape(n, HEAD_DIM)


def kernel(x, positions, norm_mix, norm_mlp, attn_w_qkv, attn_w_o, pool_w, pool_scale,
           conv_w_in, conv_w, conv_w_out, mlp_w_up, mlp_w_down, norm_final):
    batch, seq, d = x.shape
    depth = norm_mix.shape[0]
    assert d == D_MODEL and seq % ROW_TILE == 0 and ROW_TILE % MOBA_BLOCK == 0
    assert (depth - 1) % 3 == 0, "the final norm is fused into a MoBA layer's MLP stage"
    cos, sin = _rope_tables(positions)
    h = x.reshape(batch * seq, d)
    rows = lambda v: v.reshape(v.shape[0], 1, v.shape[-1])
    g_mix, g_mlp, p_scale = rows(norm_mix), rows(norm_mlp), rows(pool_scale)
    g_final = norm_final.reshape(1, 1, d)
    w_pool = pool_w.astype(BF16)
    i_attn = i_pool = i_conv = 0
    for i in range(depth):
        kind = i % 3
        mlp = ((g_mlp, i), (mlp_w_up, i), (mlp_w_down, i))
        if kind == 0:
            qt, k, vt, km = _qkv_call(h, (g_mix, i), (attn_w_qkv, i_attn), cos, sin)
            a = _attn_call(qt, k, vt, km.reshape(-1, d), batch, seq)
            h = _attn_mlp_call(h, a, (attn_w_o, i_attn), *mlp,
                               g_final=(g_final, 0) if i == depth - 1 else None)
            i_attn += 1
        elif kind == 1:
            h = _pool_mlp_call(h, (g_mix, i), (w_pool, i_pool), (p_scale, i_pool), *mlp, seq)
            i_pool += 1
        else:
            h = _conv_mlp_call(h, (g_mix, i), (conv_w_in, i_conv), (conv_w, i_conv),
                               (conv_w_out, i_conv), *mlp, seq)
            i_conv += 1
    return h.reshape(batch, seq, d)
```

```python
import functools
import math

import jax
import jax.numpy as jnp
from jax import lax
from jax.experimental import pallas as pl
from jax.experimental.pallas import tpu as pltpu

D_MODEL = 1024
N_HEADS = 8
HEAD_DIM = D_MODEL // N_HEADS
ROT_DIM = HEAD_DIM // 4
ROPE_THETA = 500000.0
MOBA_BLOCK = 256
MOBA_TOPK = 3
POOL_WINDOWS = (2, 4, 8, 16)
POOL_GROUP_DIM = D_MODEL // len(POOL_WINDOWS)
CONV_WIDTH = 3
D_FF = 4 * D_MODEL
NORM_EPS = 1e-6
NEG_INF = -1e30

HALO = 16
POOL_ROWS = 128
ROW_SLABS = 2
ROW_TILE = 512
WIDE_ROW_TILE = 1024
FF_CHUNK = 1024
ONES_ROWS = 16
STAGE_BYTES = 512 * 1024
IN_FLIGHT_BYTES = 4 * 1024 * 1024
ANY_SPEC = pl.BlockSpec(memory_space=pl.ANY)
VMEM_LIMIT = 56 * 1024 * 1024
Q_SCALE = (1.0 / math.sqrt(HEAD_DIM)) * math.log2(math.e)

F32 = jnp.float32
BF16 = jnp.bfloat16


def _resident(stacked, layer):
    tail = stacked.shape[1:]
    index = (layer,) + (0,) * len(tail)
    return pl.BlockSpec((None,) + tail, lambda *_: index, pipeline_mode=pl.Buffered(1))


def _chunk_rows(width):
    rows = STAGE_BYTES // (4 * width)
    return rows - rows % 16


def _stage_slots(n_weights):
    return -(-IN_FLIGHT_BYTES // (n_weights * STAGE_BYTES)) + 1


def _weight_scratch(weights):
    slots = _stage_slots(len(weights))
    out = []
    for w in weights:
        rows, width = w[0].shape[1:]
        out += [pltpu.VMEM((rows, width), BF16),
                pltpu.VMEM((slots, _chunk_rows(width), width), F32),
                pltpu.SemaphoreType.DMA((slots,))]
    return out


def _fetch_weights(hbm, layers, fetched):
    plans = []
    for w_hbm, layer, (w_vmem, stage, sem) in zip(hbm, layers, fetched):
        rows = w_vmem.shape[0]
        slots, r = stage.shape[:2]
        assert rows % r == 0 and rows // r >= slots

        def chunk(c, w_hbm=w_hbm, layer=layer, stage=stage, sem=sem, slots=slots, r=r):
            slot = c % slots
            return pltpu.make_async_copy(w_hbm.at[layer, pl.ds(c * r, r), :], stage.at[slot],
                                         sem.at[slot])

        plans.append((rows // r, slots, r, chunk, w_vmem, stage))

    for _, slots, _, chunk, _, _ in plans:
        for c in range(slots - 1):
            chunk(c).start()
    trips = max(p[0] for p in plans)

    def step(c, carry):
        for n, slots, r, chunk, w_vmem, stage in plans:
            def advance(n=n, slots=slots, r=r, chunk=chunk, w_vmem=w_vmem, stage=stage):
                @pl.when(c + slots - 1 < n)
                def _():
                    chunk(c + slots - 1).start()

                chunk(c).wait()
                w_vmem[pl.ds(pl.multiple_of(c * r, r), r), :] = stage[c % slots].astype(BF16)

            if n == trips:
                advance()
            else:
                pl.when(c < n)(advance)
        return carry

    lax.fori_loop(0, trips, step, 0)


def _with_weights(body, n_in, layers, n_out):
    n_w = len(layers)

    def kernel(*refs):
        ins = refs[:n_in]
        hbm = refs[n_in:n_in + n_w]
        outs = refs[n_in + n_w:n_in + n_w + n_out]
        scratch = refs[n_in + n_w + n_out:]
        fetched = [scratch[3 * k:3 * k + 3] for k in range(n_w)]

        @pl.when(pl.program_id(0) == 0)
        def _():
            _fetch_weights(hbm, layers, fetched)

        body(*ins, *[f[0] for f in fetched], *outs, *scratch[3 * n_w:])

    return kernel


def _row_slabs(tm):
    rows = tm // ROW_SLABS
    return [slice(s * rows, (s + 1) * rows) for s in range(ROW_SLABS)]


def _rms(x, g):
    ms = jnp.mean(x * x, axis=-1, keepdims=True)
    return x * lax.rsqrt(ms + NORM_EPS) * g


def _mlp_residual(x1, g_ref, wup_ref, wdn_ref):
    xn = _rms(x1, g_ref[...]).astype(BF16)
    acc = jnp.zeros_like(x1)
    for c in range(D_FF // FF_CHUNK):
        cs = slice(c * FF_CHUNK, (c + 1) * FF_CHUNK)
        h = jnp.dot(xn, wup_ref[:, cs], preferred_element_type=F32)
        h = jnp.maximum(h, 0.0)
        h = (h * h).astype(BF16)
        acc = acc + jnp.dot(h, wdn_ref[cs, :], preferred_element_type=F32)
    return x1 + acc


def _qkv_kernel(x_ref, g_ref, cos_ref, sin_ref, w_ref, qt_ref, k_ref, vt_ref, km_ref):
    tm = x_ref.shape[0]
    xn = _rms(x_ref[...], g_ref[...]).astype(BF16)
    cos = cos_ref[...]
    sin = sin_ref[...]
    lane = lax.broadcasted_iota(jnp.int32, (tm, HEAD_DIM), 1)
    first_half = lane < (ROT_DIM // 2)

    def rope(t):
        parts = []
        for h in range(N_HEADS):
            th = t[:, h * HEAD_DIM:(h + 1) * HEAD_DIM]
            partner = jnp.where(first_half,
                                pltpu.roll(th, HEAD_DIM - ROT_DIM // 2, 1),
                                pltpu.roll(th, ROT_DIM // 2, 1))
            parts.append(th * cos + partner * sin)
        return jnp.concatenate(parts, axis=1)

    q = jnp.dot(xn, w_ref[:, 0:D_MODEL], preferred_element_type=F32)
    q = rope(q) * Q_SCALE
    for b in range(tm // MOBA_BLOCK):
        qt_ref[b] = q[b * MOBA_BLOCK:(b + 1) * MOBA_BLOCK, :].T.astype(BF16)

    k = jnp.dot(xn, w_ref[:, D_MODEL:2 * D_MODEL], preferred_element_type=F32)
    k = rope(k)
    k_ref[...] = pltpu.bitcast(k.astype(BF16), jnp.uint32)
    for b in range(tm // MOBA_BLOCK):
        km_ref[b] = jnp.mean(k[b * MOBA_BLOCK:(b + 1) * MOBA_BLOCK, :], axis=0, keepdims=True)

    v = jnp.dot(xn, w_ref[:, 2 * D_MODEL:3 * D_MODEL], preferred_element_type=F32)
    for b in range(tm // MOBA_BLOCK):
        vt_ref[b] = pltpu.bitcast(v[b * MOBA_BLOCK:(b + 1) * MOBA_BLOCK, :].T.astype(BF16),
                                  jnp.uint32)


def _qkv_call(x, g, w, cos, sin):
    t = x.shape[0]
    tm = WIDE_ROW_TILE
    nb = tm // MOBA_BLOCK
    return pl.pallas_call(
        _with_weights(_qkv_kernel, n_in=4, layers=(w[1],), n_out=4),
        grid=(t // tm,),
        in_specs=[
            pl.BlockSpec((tm, D_MODEL), lambda i: (i, 0)),
            _resident(*g),
            pl.BlockSpec((tm, HEAD_DIM), lambda i: (i, 0)),
            pl.BlockSpec((tm, HEAD_DIM), lambda i: (i, 0)),
            ANY_SPEC,
        ],
        out_specs=[
            pl.BlockSpec((nb, D_MODEL, MOBA_BLOCK), lambda i: (i, 0, 0)),
            pl.BlockSpec((tm // 2, D_MODEL), lambda i: (i, 0)),
            pl.BlockSpec((nb, D_MODEL // 2, MOBA_BLOCK), lambda i: (i, 0, 0)),
            pl.BlockSpec((nb, 1, D_MODEL), lambda i: (i, 0, 0)),
        ],
        out_shape=[
            jax.ShapeDtypeStruct((t // MOBA_BLOCK, D_MODEL, MOBA_BLOCK), BF16),
            jax.ShapeDtypeStruct((t // 2, D_MODEL), jnp.uint32),
            jax.ShapeDtypeStruct((t // MOBA_BLOCK, D_MODEL // 2, MOBA_BLOCK), jnp.uint32),
            jax.ShapeDtypeStruct((t // MOBA_BLOCK, 1, D_MODEL), F32),
        ],
        scratch_shapes=_weight_scratch((w,)),
        compiler_params=pltpu.CompilerParams(
            dimension_semantics=("arbitrary",), vmem_limit_bytes=VMEM_LIMIT),
        name="moba_qkv",
    )(x, g[0], cos, sin, w[0])


def _attn_kernel(qa_ref, qb_ref, k_ref, vt_ref, km_ref, o_hbm, q2_ref, s_ref, bias_ref, stage_ref,
                 sem, *head_refs, n_blocks):
    heads = N_HEADS
    blk = MOBA_BLOCK
    acc_refs, stat_refs = head_refs[:heads], head_refs[heads:]
    b, p = pl.program_id(0), pl.program_id(1)
    tile_block = (p, n_blocks - 1 - p)
    q_refs = (qa_ref, qb_ref)
    n_past = n_blocks - 1

    def hs(h):
        return slice(h * HEAD_DIM, (h + 1) * HEAD_DIM)

    def keys(j, h):
        rows = pl.ds(pl.multiple_of(j * (blk // 2), blk // 2), blk // 2)
        return pltpu.bitcast(k_ref[rows, hs(h)], BF16)

    def step_target(t):
        tile = jnp.where(t >= tile_block[0], 1, 0)
        return tile, t - tile * tile_block[0]

    ones_rows = jnp.ones((ONES_ROWS, blk), BF16)

    def weighted_values(j, h, prob):
        vt_h = pltpu.bitcast(vt_ref[j, h * (HEAD_DIM // 2):(h + 1) * (HEAD_DIM // 2), :], BF16)
        vt_aug = jnp.concatenate([vt_h, ones_rows], axis=0)
        out = jnp.dot(vt_aug, prob.astype(BF16), preferred_element_type=F32)
        return out[:HEAD_DIM], out[HEAD_DIM:HEAD_DIM + 1]

    for tile in range(2):
        q2_ref[tile] = q_refs[tile][0]

    row = lax.broadcasted_iota(jnp.int32, (n_blocks, blk), 0)
    kpos = lax.broadcasted_iota(jnp.int32, (blk, blk), 0)
    qpos = lax.broadcasted_iota(jnp.int32, (blk, blk), 1)
    causal = kpos <= qpos
    for tile in range(2):
        i = tile_block[tile]
        q_ref = q_refs[tile]
        own = [jnp.dot(keys(i, h), q_ref[0, hs(h), :], preferred_element_type=F32)
               for h in range(heads)]
        gates = [jnp.dot(km_ref[:, hs(h)].astype(BF16), q_ref[0, hs(h), :],
                         preferred_element_type=F32) for h in range(heads)]
        valid = row < i
        for h in range(heads):
            gate = jnp.where(valid, gates[h], NEG_INF)
            bias = jnp.full((n_blocks, blk), NEG_INF, F32)
            for _ in range(MOBA_TOPK):
                top = jnp.max(gate, axis=0, keepdims=True)
                first = jnp.min(jnp.where(gate == top, row, n_blocks), axis=0, keepdims=True)
                pick = row == first
                bias = jnp.where(pick, 0.0, bias)
                gate = jnp.where(pick, -jnp.inf, gate)
            bias_ref[tile, h] = jnp.where(valid, bias, NEG_INF)
        for h in range(heads):
            s = jnp.where(causal, own[h], NEG_INF)
            m = jnp.max(s, axis=0, keepdims=True)
            pv, p_sum = weighted_values(i, h, jnp.exp2(s - m))
            acc_refs[h][tile] = pv
            stat_refs[h][tile, 0:1, :] = m
            stat_refs[h][tile, 1:2, :] = p_sum

    def scores(t, h):
        tile, j = step_target(t)
        return jnp.dot(keys(j, h), q2_ref[tile, hs(h), :],
                       preferred_element_type=F32)

    def past_block(h, slot, tile, j):
        s = s_ref[slot, h]
        bias = bias_ref[tile, h, pl.ds(j, 1), :]
        m_old = stat_refs[h][tile, 0:1, :]
        m_new = jnp.maximum(m_old, jnp.max(s, axis=0, keepdims=True) + bias)
        alpha = jnp.exp2(m_old - m_new)
        pv, p_sum = weighted_values(j, h, jnp.exp2(s - (m_new - bias)))
        acc_refs[h][tile] = alpha * acc_refs[h][tile] + pv
        stat_refs[h][tile, 0:1, :] = m_new
        stat_refs[h][tile, 1:2, :] = alpha * stat_refs[h][tile, 1:2, :] + p_sum

    for h in range(heads):
        s_ref[0, h] = scores(0, h)
    for t in range(n_past):
        slot = t % 2
        tile, j = step_target(t)
        for h in range(heads):
            past_block(h, slot, tile, j)
            if t + 1 < n_past:
                s_ref[1 - slot, h] = scores(t + 1, h)

    pairs = n_blocks // 2
    step = b * pairs + p
    slot = step % 2

    def out_copy(slot, tile, batch_idx, pair_idx):
        block = pair_idx if tile == 0 else n_blocks - 1 - pair_idx
        return pltpu.make_async_copy(stage_ref.at[slot, tile],
                                     o_hbm.at[batch_idx * n_blocks + block], sem.at[slot, tile])

    for tile in range(2):
        for h in range(heads):
            inv_l = 1.0 / stat_refs[h][tile, 1:2, :]
            stage_ref[slot, tile, hs(h), :] = (acc_refs[h][tile] * inv_l).astype(BF16)
        out_copy(slot, tile, b, p).start()

    @pl.when(step > 0)
    def _():
        prev_b = jnp.where(p == 0, b - 1, b)
        prev_p = jnp.where(p == 0, pairs - 1, p - 1)
        for tile in range(2):
            out_copy(1 - slot, tile, prev_b, prev_p).wait()

    @pl.when(step == pl.num_programs(0) * pairs - 1)
    def _():
        for tile in range(2):
            out_copy(slot, tile, b, p).wait()


def _attn_call(qt, k, vt, km, batch, seq):
    n_blocks = seq // MOBA_BLOCK
    assert n_blocks % 2 == 0
    q_block = (1, D_MODEL, MOBA_BLOCK)
    per_head = ([pltpu.VMEM((2, HEAD_DIM, MOBA_BLOCK), F32)] * N_HEADS
                + [pltpu.VMEM((2, 2, MOBA_BLOCK), F32)] * N_HEADS)
    return pl.pallas_call(
        functools.partial(_attn_kernel, n_blocks=n_blocks),
        grid=(batch, n_blocks // 2),
        in_specs=[
            pl.BlockSpec(q_block, lambda b, p: (b * n_blocks + p, 0, 0)),
            pl.BlockSpec(q_block, lambda b, p: (b * n_blocks + n_blocks - 1 - p, 0, 0)),
            pl.BlockSpec((seq // 2, D_MODEL), lambda b, p: (b, 0)),
            pl.BlockSpec((n_blocks, D_MODEL // 2, MOBA_BLOCK), lambda b, p: (b, 0, 0)),
            pl.BlockSpec((n_blocks, D_MODEL), lambda b, p: (b, 0)),
        ],
        out_specs=ANY_SPEC,
        out_shape=jax.ShapeDtypeStruct((batch * n_blocks, D_MODEL, MOBA_BLOCK), BF16),
        scratch_shapes=[
            pltpu.VMEM((2, D_MODEL, MOBA_BLOCK), BF16),
            pltpu.VMEM((2, N_HEADS, MOBA_BLOCK, MOBA_BLOCK), F32),
            pltpu.VMEM((2, N_HEADS, n_blocks, MOBA_BLOCK), F32),
            pltpu.VMEM((2, 2, D_MODEL, MOBA_BLOCK), BF16),
            pltpu.SemaphoreType.DMA((2, 2)),
        ] + per_head,
        compiler_params=pltpu.CompilerParams(
            dimension_semantics=("arbitrary", "arbitrary"), vmem_limit_bytes=VMEM_LIMIT),
        name="moba_attention",
    )(qt, qt, k, vt, km)


def _attn_mlp_kernel(x_ref, a_ref, g_ref, *rest, final):
    if final:
        gf_ref, wo_ref, wup_ref, wdn_ref, o_ref = rest
    else:
        wo_ref, wup_ref, wdn_ref, o_ref = rest
    slabs = [slice(s * MOBA_BLOCK, (s + 1) * MOBA_BLOCK) for s in range(a_ref.shape[0])]
    lhs_dim0 = (((0,), (0,)), ((), ()))
    x1 = [x_ref[rows, :] + lax.dot_general(a_ref[s], wo_ref[...], lhs_dim0,
                                           preferred_element_type=F32)
          for s, rows in enumerate(slabs)]
    for rows, x1_slab in zip(slabs, x1):
        out = _mlp_residual(x1_slab, g_ref, wup_ref, wdn_ref)
        if final:
            out = _rms(out, gf_ref[...])
        o_ref[rows, :] = out


def _attn_mlp_call(x, a, wo, g, wup, wdn, g_final=None):
    t = x.shape[0]
    tm = WIDE_ROW_TILE
    final = g_final is not None
    in_specs = [
        pl.BlockSpec((tm, D_MODEL), lambda i: (i, 0)),
        pl.BlockSpec((tm // MOBA_BLOCK, D_MODEL, MOBA_BLOCK), lambda i: (i, 0, 0)),
        _resident(*g),
    ]
    args = [x, a, g[0]]
    if final:
        in_specs.append(_resident(*g_final))
        args.append(g_final[0])
    weights = (wo, wup, wdn)
    return pl.pallas_call(
        _with_weights(functools.partial(_attn_mlp_kernel, final=final), n_in=len(args),
                      layers=[w[1] for w in weights], n_out=1),
        grid=(t // tm,),
        in_specs=in_specs + [ANY_SPEC] * len(weights),
        out_specs=pl.BlockSpec((tm, D_MODEL), lambda i: (i, 0)),
        out_shape=jax.ShapeDtypeStruct((t, D_MODEL), F32),
        scratch_shapes=_weight_scratch(weights),
        compiler_params=pltpu.CompilerParams(
            dimension_semantics=("arbitrary",), vmem_limit_bytes=VMEM_LIMIT),
        name="attn_out_mlp",
    )(*args, *[w[0] for w in weights])


def _halo_spec(tm):
    per = tm // HALO
    return pl.BlockSpec((HALO, D_MODEL), lambda i: (jnp.maximum(i * per - 1, 0), 0))


def _pool_mlp_kernel(x_ref, halo_ref, gmix_ref, pw_ref, ps_ref, g_ref, wup_ref, wdn_ref,
                     o_ref, ext_ref, *, tiles_per_seq):
    tm = x_ref.shape[0]
    tile_in_seq = pl.program_id(0) % tiles_per_seq
    x = x_ref[...]
    gmix = gmix_ref[...]
    xn = _rms(x, gmix)
    ext_ref[0:HALO, :] = jnp.where(tile_in_seq == 0, 0.0,
                                   _rms(halo_ref[...], gmix)).astype(BF16)
    ext_ref[HALO:HALO + tm, :] = xn.astype(BF16)
    pos = tile_in_seq * tm + lax.broadcasted_iota(jnp.int32, (tm, 1), 0)
    ext_rows = POOL_ROWS + HALO
    t_idx = lax.broadcasted_iota(jnp.int32, (POOL_ROWS, ext_rows), 0) + HALO
    e_idx = lax.broadcasted_iota(jnp.int32, (POOL_ROWS, ext_rows), 1)
    bands = [jnp.where(e_idx > t_idx - w, jnp.where(e_idx <= t_idx, 1.0, 0.0), 0.0).astype(BF16)
             for w in POOL_WINDOWS]
    slabs = _row_slabs(tm)
    x1 = []
    for rows in slabs:
        ys = []
        for g, w in enumerate(POOL_WINDOWS):
            cs = slice(g * POOL_GROUP_DIM, (g + 1) * POOL_GROUP_DIM)
            total = jnp.concatenate(
                [jnp.dot(bands[g], ext_ref[r0:r0 + ext_rows, cs], preferred_element_type=F32)
                 for r0 in range(rows.start, rows.stop, POOL_ROWS)], axis=0)
            count = jnp.minimum(pos[rows, :] + 1, w).astype(F32)
            pooled = total / count - xn[rows, cs]
            ys.append(jnp.dot(pooled.astype(BF16), pw_ref[g], preferred_element_type=F32))
        x1.append(x[rows, :] + jnp.concatenate(ys, axis=1) * ps_ref[...])
    for rows, x1_slab in zip(slabs, x1):
        o_ref[rows, :] = _mlp_residual(x1_slab, g_ref, wup_ref, wdn_ref)


def _pool_mlp_call(x, gmix, pw, ps, g, wup, wdn, seq):
    t = x.shape[0]
    tm = ROW_TILE
    weights = (wup, wdn)
    return pl.pallas_call(
        _with_weights(functools.partial(_pool_mlp_kernel, tiles_per_seq=seq // tm), n_in=6,
                      layers=[w[1] for w in weights], n_out=1),
        grid=(t // tm,),
        in_specs=[
            pl.BlockSpec((tm, D_MODEL), lambda i: (i, 0)),
            _halo_spec(tm),
            _resident(*gmix),
            _resident(*pw),
            _resident(*ps),
            _resident(*g),
        ] + [ANY_SPEC] * len(weights),
        out_specs=pl.BlockSpec((tm, D_MODEL), lambda i: (i, 0)),
        out_shape=jax.ShapeDtypeStruct((t, D_MODEL), F32),
        scratch_shapes=_weight_scratch(weights) + [pltpu.VMEM((HALO + tm, D_MODEL), BF16)],
        compiler_params=pltpu.CompilerParams(
            dimension_semantics=("arbitrary",), vmem_limit_bytes=VMEM_LIMIT),
        name="pool_mlp",
    )(x, x, gmix[0], pw[0], ps[0], g[0], *[w[0] for w in weights])


def _conv_mlp_kernel(x_ref, halo_ref, gmix_ref, cw_ref, g_ref, win_ref, wout_ref, wup_ref,
                     wdn_ref, o_ref, xe_ref, z_ref, *, tiles_per_seq):
    tm = x_ref.shape[0]
    tile_in_seq = pl.program_id(0) % tiles_per_seq
    x = x_ref[...]
    gmix = gmix_ref[...]
    xe_ref[0:HALO, :] = _rms(halo_ref[...], gmix).astype(BF16)
    xe_ref[HALO:HALO + tm, :] = _rms(x, gmix).astype(BF16)
    gate_c = jnp.dot(xe_ref[...], win_ref[:, D_MODEL:2 * D_MODEL], preferred_element_type=F32)
    hid = jnp.dot(xe_ref[...], win_ref[:, 2 * D_MODEL:3 * D_MODEL], preferred_element_type=F32)
    z = gate_c * hid
    z_ref[0:HALO, :] = jnp.where(tile_in_seq == 0, 0.0, z[0:HALO, :])
    z_ref[HALO:HALO + tm, :] = z[HALO:HALO + tm, :]

    cw = cw_ref[...]
    slabs = _row_slabs(tm)
    x1 = []
    for rows in slabs:
        zc = cw[CONV_WIDTH - 1:CONV_WIDTH, :] * z_ref[HALO + rows.start:HALO + rows.stop, :]
        for back in range(1, CONV_WIDTH):
            tap = cw[CONV_WIDTH - 1 - back:CONV_WIDTH - back, :]
            zc = zc + tap * z_ref[HALO - back + rows.start:HALO - back + rows.stop, :]
        gate_b = jnp.dot(xe_ref[HALO + rows.start:HALO + rows.stop, :], win_ref[:, 0:D_MODEL],
                         preferred_element_type=F32)
        y = jnp.dot((gate_b * zc).astype(BF16), wout_ref[...], preferred_element_type=F32)
        x1.append(x[rows, :] + y)
    for rows, x1_slab in zip(slabs, x1):
        o_ref[rows, :] = _mlp_residual(x1_slab, g_ref, wup_ref, wdn_ref)


def _conv_mlp_call(x, gmix, win, cw, wout, g, wup, wdn, seq):
    t = x.shape[0]
    tm = ROW_TILE
    weights = (win, wout, wup, wdn)
    return pl.pallas_call(
        _with_weights(functools.partial(_conv_mlp_kernel, tiles_per_seq=seq // tm), n_in=5,
                      layers=[w[1] for w in weights], n_out=1),
        grid=(t // tm,),
        in_specs=[
            pl.BlockSpec((tm, D_MODEL), lambda i: (i, 0)),
            _halo_spec(tm),
            _resident(*gmix),
            _resident(*cw),
            _resident(*g),
        ] + [ANY_SPEC] * len(weights),
        out_specs=pl.BlockSpec((tm, D_MODEL), lambda i: (i, 0)),
        out_shape=jax.ShapeDtypeStruct((t, D_MODEL), F32),
        scratch_shapes=_weight_scratch(weights) + [
            pltpu.VMEM((HALO + tm, D_MODEL), BF16),
            pltpu.VMEM((HALO + tm, D_MODEL), F32),
        ],
        compiler_params=pltpu.CompilerParams(
            dimension_semantics=("arbitrary",), vmem_limit_bytes=VMEM_LIMIT),
        name="conv_mlp",
    )(x, x, gmix[0], cw[0], g[0], *[w[0] for w in weights])


ROPE_PACK = HEAD_DIM // ROT_DIM


def _rope_table_kernel(pos_ref, freq_ref, sign_ref, cos_ref, sin_ref):
    ang = pos_ref[...].astype(F32) * freq_ref[...]
    c = jnp.cos(ang)
    s = jnp.sin(ang) * sign_ref[...]
    rotary = lax.broadcasted_iota(jnp.int32, c.shape, 1) < ROT_DIM
    for g in range(ROPE_PACK):
        shift = (HEAD_DIM - ROT_DIM * g) % HEAD_DIM
        cg = pltpu.roll(c, shift, 1) if shift else c
        sg = pltpu.roll(s, shift, 1) if shift else s
        cos_ref[g] = jnp.where(rotary, cg, 1.0)
        sin_ref[g] = jnp.where(rotary, sg, 0.0)


def _rope_tables(positions):
    n = positions.size
    rows = n // ROPE_PACK
    half = ROT_DIM // 2
    inv_freq = ROPE_THETA ** (-jnp.arange(0, ROT_DIM, 2, dtype=F32) / ROT_DIM)
    freq = jnp.tile(inv_freq, HEAD_DIM // half).reshape(1, HEAD_DIM)
    sign = jnp.tile(jnp.concatenate([-jnp.ones((half,), F32), jnp.ones((half,), F32)]),
                    ROPE_PACK).reshape(1, HEAD_DIM)
    packed = jnp.repeat(positions.reshape(ROPE_PACK, rows).T, ROT_DIM, axis=1)
    rt = ROW_TILE
    table = jax.ShapeDtypeStruct((ROPE_PACK, rows, HEAD_DIM), F32)
    row_spec = pl.BlockSpec((1, HEAD_DIM), lambda i: (0, 0))
    cos, sin = pl.pallas_call(
        _rope_table_kernel,
        grid=(rows // rt,),
        in_specs=[pl.BlockSpec((rt, HEAD_DIM), lambda i: (i, 0)), row_spec, row_spec],
        out_specs=[pl.BlockSpec((ROPE_PACK, rt, HEAD_DIM), lambda i: (0, i, 0))] * 2,
        out_shape=[table, table],
        compiler_params=pltpu.CompilerParams(dimension_semantics=("arbitrary",)),
        name="rope_tables",
    )(packed, freq, sign)
    return cos.reshape(n, HEAD_DIM), sin.reshape(n, HEAD_DIM)


def kernel(x, positions, norm_mix, norm_mlp, attn_w_qkv, attn_w_o, pool_w, pool_scale,
           conv_w_in, conv_w, conv_w_out, mlp_w_up, mlp_w_down, norm_final):
    batch, seq, d = x.shape
    depth = norm_mix.shape[0]
    assert d == D_MODEL and seq % ROW_TILE == 0 and ROW_TILE % MOBA_BLOCK == 0
    assert (depth - 1) % 3 == 0, "the final norm is fused into a MoBA layer's MLP stage"
    cos, sin = _rope_tables(positions)
    h = x.reshape(batch * seq, d)
    rows = lambda v: v.reshape(v.shape[0], 1, v.shape[-1])
    g_mix, g_mlp, p_scale = rows(norm_mix), rows(norm_mlp), rows(pool_scale)
    g_final = norm_final.reshape(1, 1, d)
    w_pool = pool_w.astype(BF16)
    i_attn = i_pool = i_conv = 0
    for i in range(depth):
        kind = i % 3
        mlp = ((g_mlp, i), (mlp_w_up, i), (mlp_w_down, i))
        if kind == 0:
            qt, k, vt, km = _qkv_call(h, (g_mix, i), (attn_w_qkv, i_attn), cos, sin)
            a = _attn_call(qt, k, vt, km.reshape(-1, d), batch, seq)
            h = _attn_mlp_call(h, a, (attn_w_o, i_attn), *mlp,
                               g_final=(g_final, 0) if i == depth - 1 else None)
            i_attn += 1
        elif kind == 1:
            h = _pool_mlp_call(h, (g_mix, i), (w_pool, i_pool), (p_scale, i_pool), *mlp, seq)
            i_pool += 1
        else:
            h = _conv_mlp_call(h, (g_mix, i), (conv_w_in, i_conv), (conv_w, i_conv),
                               (conv_w_out, i_conv), *mlp, seq)
            i_conv += 1
    return h.reshape(batch, seq, d)
```
